```python
import math
import jax, jax.numpy as jnp
from jax import lax
import numpy as np

D_MODEL = 2048
BATCH = 2
SEQ = 8192
DEPTH = 1

ATTN_WIDTH = D_MODEL // 2
CONV_WIDTH = D_MODEL - ATTN_WIDTH
HEAD_DIM = 64
N_Q_HEADS = ATTN_WIDTH // HEAD_DIM
N_KV_GROUPS = 4
HEADS_PER_GROUP = N_Q_HEADS // N_KV_GROUPS
KV_WIDTH = N_KV_GROUPS * HEAD_DIM
N_BRANCH = 3
GATE_WIDTH = N_Q_HEADS * N_BRANCH
IN_WIDTH = ATTN_WIDTH + 6 * KV_WIDTH + GATE_WIDTH + 2 * CONV_WIDTH
CMP_BLOCK = 32
CMP_STRIDE = 16
CMP_HIDDEN = 2 * HEAD_DIM
SEL_BLOCK = 64
N_SEL = 16
WINDOW = 512
Q_BLOCK = 128
FORCE_SCORE = 1.0e4
NEG_INF = -1.0e30
CONV_KERNEL = 31
D_FF = 4 * D_MODEL
ROPE_THETA = 10000.0
LN_EPS = 1e-5
DEEPNORM_ALPHA = (2.0 * DEPTH) ** 0.25
DEEPNORM_BETA = (8.0 * DEPTH) ** -0.25

kernel_name = "nsa_conformer_hymba_deepnorm"


def layer_norm(x, g, b):
    xf = x.astype(jnp.float32)
    mu = jnp.mean(xf, axis=-1, keepdims=True)
    var = jnp.mean(jnp.square(xf - mu), axis=-1, keepdims=True)
    y = (xf - mu) * lax.rsqrt(var + LN_EPS)
    return (y * g.astype(jnp.float32) + b.astype(jnp.float32)).astype(x.dtype)


def rope(t, pos):
    half = HEAD_DIM // 2
    inv_freq = ROPE_THETA ** (-jnp.arange(half, dtype=jnp.float32) / half)
    ang = pos.astype(jnp.float32)[:, None] * inv_freq[None, :]
    shape = (1, pos.shape[0]) + (1,) * (t.ndim - 3) + (half,)
    cos = jnp.cos(ang).reshape(shape)
    sin = jnp.sin(ang).reshape(shape)
    t1, t2 = jnp.split(t.astype(jnp.float32), 2, axis=-1)
    return jnp.concatenate([t1 * cos - t2 * sin, t1 * sin + t2 * cos], axis=-1).astype(t.dtype)


def masked_softmax(s, valid):
    s = jnp.where(valid, s.astype(jnp.float32), NEG_INF)
    p = jax.nn.softmax(s, axis=-1)
    return jnp.where(valid, p, 0.0)


def compress_blocks(t, tok_idx, pe, w1, w2):
    blocks = t[:, tok_idx] + pe[:, None, :]
    hid = jax.nn.gelu(jnp.einsum('bnlgd,lde->bnge', blocks, w1))
    return jnp.einsum('bnge,ed->bngd', hid, w2)


def sparse_attention(q, k_cmp, v_cmp, k_sel, v_sel, k_win, v_win, gates,
                     cmp_pe_k, cmp_w1_k, cmp_w2_k, cmp_pe_v, cmp_w1_v, cmp_w2_v):
    B, S = q.shape[0], q.shape[1]
    scale = HEAD_DIM ** -0.5
    n_cmp = (S - CMP_BLOCK) // CMP_STRIDE + 1
    cmp_start = jnp.arange(n_cmp) * CMP_STRIDE
    cmp_end = cmp_start + CMP_BLOCK - 1
    tok_idx = cmp_start[:, None] + jnp.arange(CMP_BLOCK)[None, :]
    kc = rope(compress_blocks(k_cmp, tok_idx, cmp_pe_k, cmp_w1_k, cmp_w2_k), cmp_end)
    vc = compress_blocks(v_cmp, tok_idx, cmp_pe_v, cmp_w1_v, cmp_w2_v)
    n_slc = S // SEL_BLOCK
    slc_start = jnp.arange(n_slc) * SEL_BLOCK
    overlap = ((cmp_start[:, None] < slc_start[None, :] + SEL_BLOCK)
               & (cmp_end[:, None] >= slc_start[None, :])).astype(jnp.float32)
    n_top = min(N_SEL, n_slc)
    k_sel_t = k_sel.transpose(0, 2, 1, 3)
    v_sel_t = v_sel.transpose(0, 2, 1, 3)
    gather = jax.vmap(jax.vmap(lambda arr, idx: arr[idx]))
    pad = ((0, 0), (WINDOW, 0), (0, 0), (0, 0))
    k_win_p = jnp.pad(k_win, pad)
    v_win_p = jnp.pad(v_win, pad)
    blk_j = jnp.arange(n_slc)

    def attend_block(qb):
        s0 = qb * Q_BLOCK
        t = s0 + jnp.arange(Q_BLOCK)
        qq = lax.dynamic_slice_in_dim(q, s0, Q_BLOCK, axis=1)
        s_c = jnp.einsum('bqghd,bngd->bghqn', qq, kc) * scale
        p_c = masked_softmax(s_c, cmp_end[None, :] <= t[:, None])
        o_c = jnp.einsum('bghqn,bngd->bqghd', p_c.astype(vc.dtype), vc)
        imp = jnp.einsum('bghqn,nj->bgqj', p_c, overlap)
        cur = (t // SEL_BLOCK)[:, None]
        valid_blk = slc_start[None, :] <= t[:, None]
        forced = (blk_j[None, :] == 0) | (blk_j[None, :] == cur) | (blk_j[None, :] == cur - 1)
        imp = jnp.where(forced, FORCE_SCORE, jnp.where(valid_blk, imp, -1.0))
        _, sel = lax.top_k(imp, n_top)
        n_tok = n_top * SEL_BLOCK
        tok = (sel[..., None] * SEL_BLOCK + jnp.arange(SEL_BLOCK)).reshape(B, N_KV_GROUPS, Q_BLOCK * n_tok)
        ks = gather(k_sel_t, tok).reshape(B, N_KV_GROUPS, Q_BLOCK, n_tok, HEAD_DIM)
        vs = gather(v_sel_t, tok).reshape(B, N_KV_GROUPS, Q_BLOCK, n_tok, HEAD_DIM)
        valid_s = tok.reshape(B, N_KV_GROUPS, Q_BLOCK, n_tok) <= t[None, None, :, None]
        s_s = jnp.einsum('bqghd,bgqkd->bghqk', qq, ks) * scale
        p_s = masked_softmax(s_s, valid_s[:, :, None])
        o_s = jnp.einsum('bghqk,bgqkd->bqghd', p_s.astype(vs.dtype), vs)
        kw = lax.dynamic_slice_in_dim(k_win_p, s0, WINDOW + Q_BLOCK, axis=1)
        vw = lax.dynamic_slice_in_dim(v_win_p, s0, WINDOW + Q_BLOCK, axis=1)
        kp = s0 - WINDOW + jnp.arange(WINDOW + Q_BLOCK)
        rel = t[:, None] - kp[None, :]
        valid_w = (kp[None, :] >= 0) & (rel >= 0) & (rel < WINDOW)
        s_w = jnp.einsum('bqghd,bkgd->bghqk', qq, kw) * scale
        p_w = masked_softmax(s_w, valid_w)
        o_w = jnp.einsum('bghqk,bkgd->bqghd', p_w.astype(vw.dtype), vw)
        g = lax.dynamic_slice_in_dim(gates, s0, Q_BLOCK, axis=1)
        return g[..., 0:1] * o_c + g[..., 1:2] * o_s + g[..., 2:3] * o_w

    out = lax.map(attend_block, jnp.arange(S // Q_BLOCK))
    return out.transpose(1, 0, 2, 3, 4, 5).reshape(B, S, ATTN_WIDTH)


def conformer_conv(glu_in, conv_w, conv_b, ln_g, ln_b):
    a, gate = jnp.split(glu_in, 2, axis=-1)
    u = a * jax.nn.sigmoid(gate)
    u = lax.conv_general_dilated(u, conv_w, window_strides=(1,), padding=[(CONV_KERNEL - 1, 0)],
                                 dimension_numbers=('NWC', 'WIO', 'NWC'),
                                 feature_group_count=CONV_WIDTH) + conv_b
    return jax.nn.silu(layer_norm(u, ln_g, ln_b))


def hybrid_mixer(x, w_in, cmp_pe_k, cmp_w1_k, cmp_w2_k, cmp_pe_v, cmp_w1_v, cmp_w2_v,
                 conv_w, conv_b, conv_ln_g, conv_ln_b, w_out):
    B, S, _ = x.shape
    proj = x @ w_in
    q, kv, gate_logits, glu_in = jnp.split(
        proj, [ATTN_WIDTH, ATTN_WIDTH + 6 * KV_WIDTH, ATTN_WIDTH + 6 * KV_WIDTH + GATE_WIDTH], axis=-1)
    pos = jnp.arange(S)
    q = rope(q.reshape(B, S, N_KV_GROUPS, HEADS_PER_GROUP, HEAD_DIM), pos)
    k_cmp, v_cmp, k_sel, v_sel, k_win, v_win = [
        t.reshape(B, S, N_KV_GROUPS, HEAD_DIM) for t in jnp.split(kv, 6, axis=-1)]
    k_sel = rope(k_sel, pos)
    k_win = rope(k_win, pos)
    gates = jax.nn.sigmoid(gate_logits.astype(jnp.float32)).astype(x.dtype).reshape(
        B, S, N_KV_GROUPS, HEADS_PER_GROUP, N_BRANCH)
    attn = sparse_attention(q, k_cmp, v_cmp, k_sel, v_sel, k_win, v_win, gates,
                            cmp_pe_k, cmp_w1_k, cmp_w2_k, cmp_pe_v, cmp_w1_v, cmp_w2_v)
    conv = conformer_conv(glu_in, conv_w, conv_b, conv_ln_g, conv_ln_b)
    return jnp.concatenate([attn, conv], axis=-1) @ w_out


def squared_relu_mlp(x, w1, w2):
    return jnp.square(jax.nn.relu(x @ w1)) @ w2


def setup_inputs(seed: int = 0) -> dict:
    key = jax.random.key(seed)
    ks = jax.random.split(key, 20)
    f32 = jnp.float32
    nrm = lambda k, shape, s: jax.random.normal(k, shape, f32) * s
    beta = DEEPNORM_BETA
    col_scale = jnp.concatenate([jnp.full((n,), s, f32) for n, s in [
        (ATTN_WIDTH, 1.0), (KV_WIDTH, 1.0), (KV_WIDTH, beta), (KV_WIDTH, 1.0), (KV_WIDTH, beta),
        (KV_WIDTH, 1.0), (KV_WIDTH, beta), (GATE_WIDTH, 1.0), (2 * CONV_WIDTH, beta)]])
    return {
        "x": jax.random.normal(ks[0], (BATCH, SEQ, D_MODEL), f32),
        "w_in": nrm(ks[1], (DEPTH, D_MODEL, IN_WIDTH), D_MODEL ** -0.5) * col_scale,
        "cmp_pe_k": nrm(ks[2], (DEPTH, CMP_BLOCK, HEAD_DIM), 0.1),
        "cmp_w1_k": nrm(ks[3], (DEPTH, CMP_BLOCK, HEAD_DIM, CMP_HIDDEN), (CMP_BLOCK * HEAD_DIM) ** -0.5),
        "cmp_w2_k": nrm(ks[4], (DEPTH, CMP_HIDDEN, HEAD_DIM), CMP_HIDDEN ** -0.5),
        "cmp_pe_v": nrm(ks[5], (DEPTH, CMP_BLOCK, HEAD_DIM), 0.1),
        "cmp_w1_v": nrm(ks[6], (DEPTH, CMP_BLOCK, HEAD_DIM, CMP_HIDDEN), (CMP_BLOCK * HEAD_DIM) ** -0.5),
        "cmp_w2_v": nrm(ks[7], (DEPTH, CMP_HIDDEN, HEAD_DIM), CMP_HIDDEN ** -0.5),
        "conv_w": nrm(ks[8], (DEPTH, CONV_KERNEL, 1, CONV_WIDTH), CONV_KERNEL ** -0.5),
        "conv_b": nrm(ks[9], (DEPTH, CONV_WIDTH), 0.01),
        "conv_ln_g": 1.0 + nrm(ks[10], (DEPTH, CONV_WIDTH), 0.01),
        "conv_ln_b": nrm(ks[11], (DEPTH, CONV_WIDTH), 0.01),
        "w_out": nrm(ks[12], (DEPTH, D_MODEL, D_MODEL), D_MODEL ** -0.5 * beta),
        "ln1_g": 1.0 + nrm(ks[13], (DEPTH, D_MODEL), 0.01),
        "ln1_b": nrm(ks[14], (DEPTH, D_MODEL), 0.01),
        "w_ff1": nrm(ks[15], (DEPTH, D_MODEL, D_FF), D_MODEL ** -0.5 * beta),
        "w_ff2": nrm(ks[16], (DEPTH, D_FF, D_MODEL), D_FF ** -0.5 * beta),
        "ln2_g": 1.0 + nrm(ks[17], (DEPTH, D_MODEL), 0.01),
        "ln2_b": nrm(ks[18], (DEPTH, D_MODEL), 0.01),
    }


def reference(x, w_in, cmp_pe_k, cmp_w1_k, cmp_w2_k, cmp_pe_v, cmp_w1_v, cmp_w2_v,
              conv_w, conv_b, conv_ln_g, conv_ln_b, w_out, ln1_g, ln1_b,
              w_ff1, w_ff2, ln2_g, ln2_b):
    for l in range(DEPTH):
        mix = hybrid_mixer(x, w_in[l], cmp_pe_k[l], cmp_w1_k[l], cmp_w2_k[l],
                           cmp_pe_v[l], cmp_w1_v[l], cmp_w2_v[l],
                           conv_w[l], conv_b[l], conv_ln_g[l], conv_ln_b[l], w_out[l])
        x = layer_norm(DEEPNORM_ALPHA * x + mix, ln1_g[l], ln1_b[l])
        x = layer_norm(DEEPNORM_ALPHA * x + squared_relu_mlp(x, w_ff1[l], w_ff2[l]), ln2_g[l], ln2_b[l])
    return x
```

```python
import functools

import jax
import jax.numpy as jnp
from jax import lax
from jax.experimental import pallas as pl
from jax.experimental.pallas import tpu as pltpu

F32 = jnp.float32
BF16 = jnp.bfloat16

HEAD_DIM = 64
N_KV_GROUPS = 4
HEADS_PER_GROUP = 4
N_Q_HEADS = N_KV_GROUPS * HEADS_PER_GROUP
N_BRANCH = 3
KV_WIDTH = N_KV_GROUPS * HEAD_DIM
CMP_BLOCK = 32
CMP_STRIDE = 16
SEL_BLOCK = 64
N_SEL = 16
WINDOW = 512
Q_BLOCK = 128
FORCE_SCORE = 1.0e4
NEG_INF = -1.0e30
MASK_BIAS = -(2.0 ** 100)
CONV_KERNEL = 31
CONV_HALO = 32
ROPE_THETA = 10000.0
LN_EPS = 1e-5
SEL_LANES = 128
GATES_PER_GROUP = HEADS_PER_GROUP * N_BRANCH
VMEM_LIMIT_BYTES = 56 * 1024 * 1024


def _cparams(n_axes):
    return pltpu.CompilerParams(dimension_semantics=("parallel",) * n_axes,
                                vmem_limit_bytes=VMEM_LIMIT_BYTES)


def _resident(shape):
    nd = len(shape)
    return pl.BlockSpec(shape, lambda *_: (0,) * nd, pipeline_mode=pl.Buffered(1))


def _dot(a, b):
    return jnp.dot(a, b, preferred_element_type=F32)


def _dot_f32(a, b):
    return jnp.dot(a, b, preferred_element_type=F32, precision=lax.Precision.HIGHEST)


def _dot_nt(a, b):
    return lax.dot_general(a, b, (((1,), (1,)), ((), ())), preferred_element_type=F32)


def _layer_norm(y, g, b):
    mu = jnp.mean(y, axis=-1, keepdims=True)
    d = y - mu
    var = jnp.mean(d * d, axis=-1, keepdims=True)
    return d * lax.rsqrt(var + LN_EPS) * g + b


def _in_proj_kernel(x_ref, wq_ref, wkv_ref, wg_ref, wa_ref, wgt_ref, cos_ref, sin_ref,
                    q_ref, kc_ref, vc_ref, ks_ref, vs_ref, kw_ref, vw_ref, g_ref, u_ref,
                    *, tm, n_row_tiles):
    xb = x_ref[...].astype(BF16)
    cos = cos_ref[...]
    sin = sin_ref[...]
    lane = lax.broadcasted_iota(jnp.int32, (tm, 128), 1)
    first_half = (lane & (HEAD_DIM - 1)) < (HEAD_DIM // 2)

    def rope(t):
        partner = jnp.where(first_half, pltpu.roll(t, 128 - HEAD_DIM // 2, 1),
                            pltpu.roll(t, HEAD_DIM // 2, 1))
        return t * cos + partner * sin

    scale = HEAD_DIM ** -0.5
    for c in range(wq_ref.shape[1] // 512):
        t = _dot(xb, wq_ref[:, c * 512:(c + 1) * 512])
        for cc in range(4):
            lo = c * 512 + cc * 128
            q_ref[:, lo:lo + 128] = (rope(t[:, cc * 128:(cc + 1) * 128]) * scale).astype(BF16)

    kv = [_dot(xb, wkv_ref[:, j * KV_WIDTH:(j + 1) * KV_WIDTH]) for j in range(6)]
    k_cmp, v_cmp, k_sel, v_sel, k_win, v_win = kv
    k_sel = jnp.concatenate([rope(k_sel[:, :128]), rope(k_sel[:, 128:])], axis=1)
    k_win = jnp.concatenate([rope(k_win[:, :128]), rope(k_win[:, 128:])], axis=1)

    s_start = (pl.program_id(0) % n_row_tiles) * tm
    key_blk = (s_start + lax.broadcasted_iota(jnp.int32, (tm, SEL_LANES), 0)) // SEL_BLOCK
    onehot = (key_blk == lax.broadcasted_iota(jnp.int32, (tm, SEL_LANES), 1)).astype(BF16)
    ones = jnp.ones((tm, HEAD_DIM), BF16)
    for g in range(N_KV_GROUPS):
        sl = slice(g * HEAD_DIM, (g + 1) * HEAD_DIM)
        kc_ref[g] = k_cmp[:, sl]
        vc_ref[g] = v_cmp[:, sl]
        ks_ref[g, :, 0:HEAD_DIM] = k_sel[:, sl].astype(BF16)
        ks_ref[g, :, HEAD_DIM:] = onehot
        vs_ref[g, :, 0:HEAD_DIM] = v_sel[:, sl].astype(BF16)
        vs_ref[g, :, HEAD_DIM:] = ones
        kw_ref[g] = k_win[:, sl].astype(BF16)
        vw_ref[g] = v_win[:, sl].astype(BF16)

    gates = jax.nn.sigmoid(_dot(xb, wg_ref[...]))
    for g in range(N_KV_GROUPS):
        g_ref[g] = gates[:, g * GATES_PER_GROUP:(g + 1) * GATES_PER_GROUP]

    for c in range(wa_ref.shape[1] // 512):
        sl = slice(c * 512, (c + 1) * 512)
        a = _dot(xb, wa_ref[:, sl])
        gt = _dot(xb, wgt_ref[:, sl])
        u_ref[:, sl] = a * jax.nn.sigmoid(gt)


def _in_proj(x2d, wq, wkv, wg, wa, wgt, cos_t, sin_t, *, batch, seq, tm):
    T, d_model = x2d.shape
    n_row_tiles = seq // tm
    G = N_KV_GROUPS
    row = lambda i: (i, 0)
    tab = lambda i: (i % n_row_tiles, 0)
    per_group = lambda i: (i // n_row_tiles, 0, i % n_row_tiles, 0)

    def grp(width, dtype):
        return (jax.ShapeDtypeStruct((batch, G, seq, width), dtype),
                pl.BlockSpec((None, G, tm, width), per_group))

    outs = [
        (jax.ShapeDtypeStruct((T, wq.shape[1]), BF16), pl.BlockSpec((tm, wq.shape[1]), row)),
        grp(HEAD_DIM, F32), grp(HEAD_DIM, F32),
        grp(HEAD_DIM + SEL_LANES, BF16), grp(2 * HEAD_DIM, BF16),
        grp(HEAD_DIM, BF16), grp(HEAD_DIM, BF16),
        grp(GATES_PER_GROUP, F32),
        (jax.ShapeDtypeStruct((T, wa.shape[1]), F32), pl.BlockSpec((tm, wa.shape[1]), row)),
    ]
    return pl.pallas_call(
        functools.partial(_in_proj_kernel, tm=tm, n_row_tiles=n_row_tiles),
        grid=(T // tm,),
        in_specs=[pl.BlockSpec((tm, d_model), row),
                  _resident(wq.shape), _resident(wkv.shape), _resident(wg.shape),
                  _resident(wa.shape), _resident(wgt.shape),
                  pl.BlockSpec((tm, 128), tab), pl.BlockSpec((tm, 128), tab)],
        out_specs=[o[1] for o in outs],
        out_shape=[o[0] for o in outs],
        compiler_params=_cparams(1),
        name="in_proj",
    )(x2d, wq, wkv, wg, wa, wgt, cos_t, sin_t)


def _compress_kernel(c_ref, pe_ref, w1a_ref, w1b_ref, w2_ref, w2s_ref, cos_ref, sin_ref,
                     o_ref, *, rotate):
    c = c_ref[...]
    n_chunks = c.shape[0]
    a = _dot_f32(c + pe_ref[0:1, :], w1a_ref[...])
    b = _dot_f32(c + pe_ref[1:2, :], w1b_ref[...])
    hid = jax.nn.gelu(a + pltpu.roll(b, n_chunks - 1, 0))
    out = _dot_f32(hid, w2_ref[...])
    if rotate:
        out = out * cos_ref[...] + _dot_f32(hid, w2s_ref[...]) * sin_ref[...]
    o_ref[...] = out.astype(o_ref.dtype)


def _compress(chunks, pe2, w1a, w1b, w2, w2s, cos_c, sin_c, *, rotate):
    BG, n_chunks, cw = chunks.shape
    return pl.pallas_call(
        functools.partial(_compress_kernel, rotate=rotate),
        grid=(BG,),
        in_specs=[pl.BlockSpec((None, n_chunks, cw), lambda i: (i, 0, 0)),
                  _resident(pe2.shape), _resident(w1a.shape), _resident(w1b.shape),
                  _resident(w2.shape), _resident(w2s.shape),
                  _resident(cos_c.shape), _resident(sin_c.shape)],
        out_specs=pl.BlockSpec((None, n_chunks, HEAD_DIM), lambda i: (i, 0, 0)),
        out_shape=jax.ShapeDtypeStruct((BG, n_chunks, HEAD_DIM), BF16),
        compiler_params=_cparams(1),
        name="compress_k" if rotate else "compress_v",
    )(chunks, pe2, w1a, w1b, w2, w2s, cos_c, sin_c)


def _stack_heads(qt):
    return jnp.concatenate(
        [qt[:, h * HEAD_DIM:(h + 1) * HEAD_DIM] for h in range(HEADS_PER_GROUP)], axis=0)


def _unstack_heads(o):
    return jnp.concatenate(
        [o[h * Q_BLOCK:(h + 1) * Q_BLOCK] for h in range(HEADS_PER_GROUP)], axis=1)


def _cmp_attn_kernel(q_ref, kc_ref, vc_ref, ov_ref, oc_ref, sb_ref, *, n_top):
    s0 = pl.program_id(1) * Q_BLOCK
    rows = HEADS_PER_GROUP * Q_BLOCK
    qs = _stack_heads(q_ref[...])
    kc = kc_ref[...]
    n_cmp = kc.shape[0]
    s = _dot_nt(qs, kc)
    t = s0 + (lax.broadcasted_iota(jnp.int32, (rows, n_cmp), 0) & (Q_BLOCK - 1))
    cmp_end = lax.broadcasted_iota(jnp.int32, (rows, n_cmp), 1) * CMP_STRIDE + (CMP_BLOCK - 1)
    valid = cmp_end <= t
    s = jnp.where(valid, s, NEG_INF)
    m = jnp.max(s, axis=-1, keepdims=True)
    e = jnp.where(valid, jnp.exp(s - m), 0.0)
    den = jnp.sum(e, axis=-1, keepdims=True)
    p = e / jnp.maximum(den, 1e-30)
    oc_ref[...] = _unstack_heads(_dot(p.astype(BF16), vc_ref[...]))

    p_sum = p[0:Q_BLOCK]
    for h in range(1, HEADS_PER_GROUP):
        p_sum = p_sum + p[h * Q_BLOCK:(h + 1) * Q_BLOCK]
    imp = _dot_f32(p_sum, ov_ref[...])
    tq = s0 + lax.broadcasted_iota(jnp.int32, (Q_BLOCK, SEL_LANES), 0)
    blk = lax.broadcasted_iota(jnp.int32, (Q_BLOCK, SEL_LANES), 1)
    cur = tq // SEL_BLOCK
    forced = (blk == 0) | (blk == cur) | (blk == cur - 1)
    imp = jnp.where(forced, FORCE_SCORE, jnp.where(blk * SEL_BLOCK <= tq, imp, -1.0))

    v = imp.T
    idx = lax.broadcasted_iota(jnp.int32, (SEL_LANES, Q_BLOCK), 0).astype(F32)
    sel = jnp.zeros((SEL_LANES, Q_BLOCK), jnp.bool_)
    for _ in range(n_top):
        vmax = jnp.max(v, axis=0, keepdims=True)
        first = jnp.min(jnp.where(v == vmax, idx, float(SEL_LANES)), axis=0, keepdims=True)
        pick = idx == first
        sel = sel | pick
        v = jnp.where(pick, -jnp.inf, v)
    sb_ref[...] = jnp.where(sel, 0.0, MASK_BIAS).T.astype(BF16)


def _cmp_attn(q, kc, vc, overlap, *, batch, seq, n_top):
    BG, n_cmp, _ = kc.shape
    G = N_KV_GROUPS
    nqb = seq // Q_BLOCK
    qmap = lambda bg, qb: ((bg // G) * nqb + qb, bg % G)
    return pl.pallas_call(
        functools.partial(_cmp_attn_kernel, n_top=n_top),
        grid=(BG, nqb),
        in_specs=[pl.BlockSpec((Q_BLOCK, HEADS_PER_GROUP * HEAD_DIM), qmap),
                  pl.BlockSpec((None, n_cmp, HEAD_DIM), lambda bg, qb: (bg, 0, 0)),
                  pl.BlockSpec((None, n_cmp, HEAD_DIM), lambda bg, qb: (bg, 0, 0)),
                  _resident(overlap.shape)],
        out_specs=[pl.BlockSpec((Q_BLOCK, HEADS_PER_GROUP * HEAD_DIM), qmap),
                   pl.BlockSpec((None, Q_BLOCK, SEL_LANES), lambda bg, qb: (bg, qb, 0))],
        out_shape=[jax.ShapeDtypeStruct((batch * seq, N_Q_HEADS * HEAD_DIM), F32),
                   jax.ShapeDtypeStruct((BG, seq, SEL_LANES), BF16)],
        compiler_params=_cparams(2),
        name="cmp_attn_select",
    )(q, kc, vc, overlap)


def _sel_win_kernel(q_ref, sb_ref, ks_ref, vs_ref, kw_ref, vw_ref, oc_ref, g_ref, o_ref,
                    *, key_tile):
    qb = pl.program_id(1)
    s0 = qb * Q_BLOCK
    rows = HEADS_PER_GROUP * Q_BLOCK
    qt = q_ref[...]
    bias = sb_ref[...]
    q_aug = jnp.concatenate(
        [jnp.concatenate([qt[:, h * HEAD_DIM:(h + 1) * HEAD_DIM], bias], axis=1)
         for h in range(HEADS_PER_GROUP)], axis=0)
    t_col = s0 + (lax.broadcasted_iota(jnp.int32, (rows, 1), 0) & (Q_BLOCK - 1))

    def step(kt, carry, causal):
        m, acc = carry
        start = pl.multiple_of(kt * key_tile, key_tile)
        s = _dot_nt(q_aug, ks_ref[pl.ds(start, key_tile), :])
        if causal:
            kp = start + lax.broadcasted_iota(jnp.int32, (rows, key_tile), 1)
            s = jnp.where(kp <= t_col, s, MASK_BIAS)
        m_new = jnp.maximum(m, jnp.max(s, axis=-1, keepdims=True))
        p = jnp.exp(s - m_new).astype(BF16)
        acc = jnp.exp(m - m_new) * acc + _dot(p, vs_ref[pl.ds(start, key_tile), :])
        return m_new, acc

    last = (s0 + Q_BLOCK - 1) // key_tile
    init = (jnp.full((rows, 1), NEG_INF, F32), jnp.zeros((rows, 2 * HEAD_DIM), F32))
    carry = lax.fori_loop(0, last, lambda kt, c: step(kt, c, False), init)
    _, acc = step(last, carry, True)
    o_s = acc[:, :HEAD_DIM] / acc[:, HEAD_DIM:HEAD_DIM + 1]

    span = WINDOW + Q_BLOCK
    wstart = pl.multiple_of(jnp.maximum(s0 - WINDOW, 0), Q_BLOCK)
    qs = _stack_heads(qt)
    sw = _dot_nt(qs, kw_ref[pl.ds(wstart, span), :])
    rel = t_col - (wstart + lax.broadcasted_iota(jnp.int32, (rows, span), 1))
    valid = (rel >= 0) & (rel < WINDOW)
    sw = jnp.where(valid, sw, NEG_INF)
    e = jnp.where(valid, jnp.exp(sw - jnp.max(sw, axis=-1, keepdims=True)), 0.0)
    pw = e / jnp.sum(e, axis=-1, keepdims=True)
    o_w = _dot(pw.astype(BF16), vw_ref[pl.ds(wstart, span), :])

    gates = g_ref[...]
    oc = oc_ref[...]
    outs = []
    for h in range(HEADS_PER_GROUP):
        r = slice(h * Q_BLOCK, (h + 1) * Q_BLOCK)
        gate = lambda br: gates[:, h * N_BRANCH + br:h * N_BRANCH + br + 1]
        outs.append(gate(0) * oc[:, h * HEAD_DIM:(h + 1) * HEAD_DIM]
                    + gate(1) * o_s[r] + gate(2) * o_w[r])
    o_ref[...] = jnp.concatenate(outs, axis=1).astype(o_ref.dtype)


def _sel_win_attn(q, sel_bias, ks_aug, vs_aug, kw, vw, o_cmp, gates, *, batch, seq, key_tile):
    G = N_KV_GROUPS
    BG = batch * G
    nqb = seq // Q_BLOCK
    qmap = lambda bg, qb: ((bg // G) * nqb + qb, bg % G)
    whole = lambda a: pl.BlockSpec((None,) + a.shape[1:], lambda bg, qb: (bg, 0, 0))
    return pl.pallas_call(
        functools.partial(_sel_win_kernel, key_tile=key_tile),
        grid=(BG, nqb),
        in_specs=[pl.BlockSpec((Q_BLOCK, HEADS_PER_GROUP * HEAD_DIM), qmap),
                  pl.BlockSpec((None, Q_BLOCK, SEL_LANES), lambda bg, qb: (bg, qb, 0)),
                  whole(ks_aug), whole(vs_aug), whole(kw), whole(vw),
                  pl.BlockSpec((Q_BLOCK, HEADS_PER_GROUP * HEAD_DIM), qmap),
                  pl.BlockSpec((None, Q_BLOCK, GATES_PER_GROUP), lambda bg, qb: (bg, qb, 0))],
        out_specs=pl.BlockSpec((Q_BLOCK, HEADS_PER_GROUP * HEAD_DIM), qmap),
        out_shape=jax.ShapeDtypeStruct((batch * seq, N_Q_HEADS * HEAD_DIM), BF16),
        compiler_params=_cparams(2),
        name="sel_win_attn",
    )(q, sel_bias, ks_aug, vs_aug, kw, vw, o_cmp, gates)


def _conv_kernel(u_ref, halo_ref, w_ref, b_ref, g_ref, beta_ref, o_ref, win_ref,
                 *, tm, n_row_tiles):
    first = (pl.program_id(0) % n_row_tiles) == 0
    halo = halo_ref[...]
    win_ref[0:CONV_HALO, :] = jnp.where(first, 0.0, halo)
    win_ref[CONV_HALO:, :] = u_ref[...]
    width = u_ref.shape[1]
    cols = []
    for c in range(width // 128):
        sl = slice(c * 128, (c + 1) * 128)
        acc = jnp.zeros((tm, 128), F32)
        for j in range(CONV_KERNEL):
            off = CONV_HALO - (CONV_KERNEL - 1) + j
            acc = acc + win_ref[off:off + tm, sl] * w_ref[j:j + 1, sl]
        cols.append(acc)
    y = jnp.concatenate(cols, axis=1) + b_ref[...]
    y = _layer_norm(y, g_ref[...], beta_ref[...])
    o_ref[...] = (y * jax.nn.sigmoid(y)).astype(o_ref.dtype)


def _conformer_conv(u, w, b, g, beta, *, seq, tm):
    T, width = u.shape
    n_row_tiles = seq // tm
    halo_blocks = tm // CONV_HALO
    return pl.pallas_call(
        functools.partial(_conv_kernel, tm=tm, n_row_tiles=n_row_tiles),
        grid=(T // tm,),
        in_specs=[pl.BlockSpec((tm, width), lambda i: (i, 0)),
                  pl.BlockSpec((CONV_HALO, width),
                               lambda i: (jnp.maximum(i * halo_blocks - 1, 0), 0)),
                  _resident(w.shape), _resident(b.shape), _resident(g.shape),
                  _resident(beta.shape)],
        out_specs=pl.BlockSpec((tm, width), lambda i: (i, 0)),
        out_shape=jax.ShapeDtypeStruct((T, width), BF16),
        scratch_shapes=[pltpu.VMEM((CONV_HALO + tm, width), F32)],
        compiler_params=_cparams(1),
        name="conformer_conv",
    )(u, u, w, b, g, beta)


def _out_proj_kernel(a_ref, c_ref, x_ref, wa_ref, wc_ref, g_ref, b_ref, o_ref, *, alpha):
    mix = _dot(a_ref[...], wa_ref[...]) + _dot(c_ref[...], wc_ref[...])
    o_ref[...] = _layer_norm(alpha * x_ref[...] + mix, g_ref[...], b_ref[...])


def _out_proj(attn, conv, x2d, w_attn, w_conv, g, b, *, alpha, tm):
    T, d_model = x2d.shape
    row = lambda i: (i, 0)
    return pl.pallas_call(
        functools.partial(_out_proj_kernel, alpha=alpha),
        grid=(T // tm,),
        in_specs=[pl.BlockSpec((tm, attn.shape[1]), row), pl.BlockSpec((tm, conv.shape[1]), row),
                  pl.BlockSpec((tm, d_model), row),
                  _resident(w_attn.shape), _resident(w_conv.shape),
                  _resident(g.shape), _resident(b.shape)],
        out_specs=pl.BlockSpec((tm, d_model), row),
        out_shape=jax.ShapeDtypeStruct((T, d_model), F32),
        compiler_params=_cparams(1),
        name="out_proj_deepnorm",
    )(attn, conv, x2d, w_attn, w_conv, g, b)


def _ffn_kernel(h_ref, w1_ref, w2_ref, g_ref, b_ref, o_ref, hb_ref, *, alpha):
    j = pl.program_id(1)

    @pl.when(j == 0)
    def _():
        hb_ref[...] = h_ref[...].astype(BF16)
        o_ref[...] = jnp.zeros_like(o_ref)

    a = jnp.maximum(_dot(hb_ref[...], w1_ref[...]), 0.0)
    o_ref[...] += _dot((a * a).astype(BF16), w2_ref[...])

    @pl.when(j == pl.num_programs(1) - 1)
    def _():
        o_ref[...] = _layer_norm(alpha * h_ref[...] + o_ref[...], g_ref[...], b_ref[...])


def _ffn(h, w1, w2, g, b, *, alpha, tm, tf):
    T, d_model = h.shape
    d_ff = w1.shape[1]
    return pl.pallas_call(
        functools.partial(_ffn_kernel, alpha=alpha),
        grid=(T // tm, d_ff // tf),
        in_specs=[pl.BlockSpec((tm, d_model), lambda i, j: (i, 0)),
                  pl.BlockSpec((d_model, tf), lambda i, j: (0, j)),
                  pl.BlockSpec((tf, d_model), lambda i, j: (j, 0)),
                  _resident(g.shape), _resident(b.shape)],
        out_specs=pl.BlockSpec((tm, d_model), lambda i, j: (i, 0)),
        out_shape=jax.ShapeDtypeStruct((T, d_model), F32),
        scratch_shapes=[pltpu.VMEM((tm, d_model), BF16)],
        compiler_params=pltpu.CompilerParams(dimension_semantics=("parallel", "arbitrary"),
                                             vmem_limit_bytes=VMEM_LIMIT_BYTES),
        name="ffn_deepnorm",
    )(h, w1, w2, g, b)


def _rope_tables(pos, reps):
    half = HEAD_DIM // 2
    inv_freq = ROPE_THETA ** (-jnp.arange(half, dtype=F32) / half)
    ang = pos.astype(F32)[:, None] * inv_freq[None, :]
    cos, sin = jnp.cos(ang), jnp.sin(ang)
    return jnp.tile(jnp.concatenate([cos, cos], axis=1), (1, reps)), \
        jnp.tile(jnp.concatenate([-sin, sin], axis=1), (1, reps))


def _tiles(seq):
    return dict(in_proj=min(512, seq), conv=min(128, seq), out_proj=min(512, seq),
                ffn_rows=min(512, seq), ffn_cols=512, key_tile=min(512, seq))


def _mixer(x2d, w_in, cmp_pe_k, cmp_w1_k, cmp_w2_k, cmp_pe_v, cmp_w1_v, cmp_w2_v,
           conv_w, conv_b, conv_ln_g, conv_ln_b, *, batch, seq):
    tiles = _tiles(seq)
    attn_w = N_Q_HEADS * HEAD_DIM
    kv_end = attn_w + 6 * KV_WIDTH
    gate_end = kv_end + N_Q_HEADS * N_BRANCH
    conv_width = (w_in.shape[1] - gate_end) // 2
    wq = w_in[:, :attn_w].astype(BF16)
    wkv = w_in[:, attn_w:kv_end].astype(BF16)
    wg = jnp.pad(w_in[:, kv_end:gate_end], ((0, 0), (0, 128 - (gate_end - kv_end)))).astype(BF16)
    wa = w_in[:, gate_end:gate_end + conv_width].astype(BF16)
    wgt = w_in[:, gate_end + conv_width:].astype(BF16)
    cos_t, sin_t = _rope_tables(jnp.arange(seq), 2)

    q, k_cmp, v_cmp, ks_aug, vs_aug, k_win, v_win, gates, u = _in_proj(
        x2d, wq, wkv, wg, wa, wgt, cos_t, sin_t, batch=batch, seq=seq, tm=tiles["in_proj"])

    BG = batch * N_KV_GROUPS
    n_chunks = seq // CMP_STRIDE
    chunk_w = CMP_STRIDE * HEAD_DIM
    cmp_end = jnp.arange(n_chunks) * CMP_STRIDE + CMP_BLOCK - 1
    cos_c, sin_c = _rope_tables(cmp_end, 1)
    swap = lambda w2: jnp.concatenate([w2[:, HEAD_DIM // 2:], w2[:, :HEAD_DIM // 2]], axis=1)

    def compress(t, pe, w1, w2, rotate):
        return _compress(t.reshape(BG, n_chunks, chunk_w), pe.reshape(2, chunk_w),
                         w1[:CMP_STRIDE].reshape(chunk_w, -1), w1[CMP_STRIDE:].reshape(chunk_w, -1),
                         w2, swap(w2), cos_c, sin_c, rotate=rotate)

    kc = compress(k_cmp, cmp_pe_k, cmp_w1_k, cmp_w2_k, True)
    vc = compress(v_cmp, cmp_pe_v, cmp_w1_v, cmp_w2_v, False)

    n_slc = seq // SEL_BLOCK
    cmp_start = jnp.arange(n_chunks) * CMP_STRIDE
    slc_start = jnp.arange(SEL_LANES) * SEL_BLOCK
    overlap = ((cmp_start[:, None] < slc_start[None, :] + SEL_BLOCK)
               & (cmp_start[:, None] + CMP_BLOCK - 1 >= slc_start[None, :])
               & (jnp.arange(n_chunks)[:, None] < n_chunks - 1)
               & (jnp.arange(SEL_LANES)[None, :] < n_slc)).astype(F32)
    o_cmp, sel_bias = _cmp_attn(q, kc, vc, overlap, batch=batch, seq=seq,
                                n_top=min(N_SEL, n_slc))

    flat = lambda a: a.reshape((BG,) + a.shape[2:])
    attn = _sel_win_attn(q, sel_bias, flat(ks_aug), flat(vs_aug), flat(k_win), flat(v_win),
                         o_cmp, flat(gates), batch=batch, seq=seq, key_tile=tiles["key_tile"])

    conv = _conformer_conv(u, conv_w.reshape(CONV_KERNEL, conv_width), conv_b.reshape(1, -1),
                           conv_ln_g.reshape(1, -1), conv_ln_b.reshape(1, -1),
                           seq=seq, tm=tiles["conv"])
    return attn, conv


def kernel(x, w_in, cmp_pe_k, cmp_w1_k, cmp_w2_k, cmp_pe_v, cmp_w1_v, cmp_w2_v, conv_w, conv_b,
           conv_ln_g, conv_ln_b, w_out, ln1_g, ln1_b, w_ff1, w_ff2, ln2_g, ln2_b):
    batch, seq, d_model = x.shape
    depth = w_in.shape[0]
    alpha = (2.0 * depth) ** 0.25
    tiles = _tiles(seq)
    attn_w = N_Q_HEADS * HEAD_DIM
    h = x.reshape(batch * seq, d_model)
    for l in range(depth):
        attn, conv = _mixer(h, w_in[l], cmp_pe_k[l], cmp_w1_k[l], cmp_w2_k[l],
                            cmp_pe_v[l], cmp_w1_v[l], cmp_w2_v[l],
                            conv_w[l], conv_b[l], conv_ln_g[l], conv_ln_b[l],
                            batch=batch, seq=seq)
        h = _out_proj(attn, conv, h, w_out[l, :attn_w].astype(BF16), w_out[l, attn_w:].astype(BF16),
                      ln1_g[l].reshape(1, -1), ln1_b[l].reshape(1, -1),
                      alpha=alpha, tm=tiles["out_proj"])
        h = _ffn(h, w_ff1[l].astype(BF16), w_ff2[l].astype(BF16),
                 ln2_g[l].reshape(1, -1), ln2_b[l].reshape(1, -1),
                 alpha=alpha, tm=tiles["ffn_rows"], tf=tiles["ffn_cols"])
    return h.reshape(batch, seq, d_model)
```

```python
import functools
import math

import jax
import jax.numpy as jnp
from jax import lax
from jax.experimental import pallas as pl
from jax.experimental.pallas import tpu as pltpu

F32 = jnp.float32
BF16 = jnp.bfloat16

HEAD_DIM = 64
N_KV_GROUPS = 4
HEADS_PER_GROUP = 4
N_Q_HEADS = N_KV_GROUPS * HEADS_PER_GROUP
GROUP_WIDTH = HEADS_PER_GROUP * HEAD_DIM
N_BRANCH = 3
KV_WIDTH = N_KV_GROUPS * HEAD_DIM
CMP_BLOCK = 32
CMP_STRIDE = 16
SEL_BLOCK = 64
N_SEL = 16
WINDOW = 512
FORCE_SCORE = 1.0e4
NEG_INF = -1.0e30
MASK_BIAS = -(2.0 ** 100)
CONV_KERNEL = 31
CONV_HALO = 32
ROPE_THETA = 10000.0
LN_EPS = 1e-5
SEL_LANES = 128
CMP_Q_BLOCK = 128
SEL_Q_BLOCK = 256
GATES_PER_GROUP = HEADS_PER_GROUP * N_BRANCH
VMEM_LIMIT_BYTES = 56 * 1024 * 1024


def _cparams(n_axes):
    return pltpu.CompilerParams(dimension_semantics=("parallel",) * n_axes,
                                vmem_limit_bytes=VMEM_LIMIT_BYTES)


def _resident(shape):
    nd = len(shape)
    return pl.BlockSpec(shape, lambda *_: (0,) * nd, pipeline_mode=pl.Buffered(1))


def _dot(a, b):
    return jnp.dot(a, b, preferred_element_type=F32)


def _dot_f32(a, b):
    return jnp.dot(a, b, preferred_element_type=F32, precision=lax.Precision.HIGHEST)


def _layer_norm(y, g, b):
    mu = jnp.mean(y, axis=-1, keepdims=True)
    d = y - mu
    var = jnp.mean(d * d, axis=-1, keepdims=True)
    return d * lax.rsqrt(var + LN_EPS) * g + b


def _in_proj_kernel(x_ref, wq_ref, wkv_ref, wg_ref, wa_ref, wgt_ref, cos_ref, sin_ref,
                    qT_ref, kc_ref, vc_ref, ks_ref, vsT_ref, kw_ref, vwT_ref, gT_ref, u_ref,
                    *, tm, n_row_tiles):
    xb = x_ref[...].astype(BF16)
    cos = cos_ref[...]
    sin = sin_ref[...]
    lane = lax.broadcasted_iota(jnp.int32, (tm, 128), 1)
    first_half = (lane & (HEAD_DIM - 1)) < (HEAD_DIM // 2)

    def rope(t):
        partner = jnp.where(first_half, pltpu.roll(t, 128 - HEAD_DIM // 2, 1),
                            pltpu.roll(t, HEAD_DIM // 2, 1))
        return t * cos + partner * sin

    q_scale = HEAD_DIM ** -0.5 * math.log2(math.e)
    for c in range(wq_ref.shape[1] // 512):
        t = _dot(xb, wq_ref[:, c * 512:(c + 1) * 512])
        t = jnp.concatenate([rope(t[:, cc * 128:(cc + 1) * 128]) for cc in range(4)], axis=1)
        qT_ref[c * 512:(c + 1) * 512, :] = (t * q_scale).T.astype(BF16)

    kv = [_dot(xb, wkv_ref[:, j * KV_WIDTH:(j + 1) * KV_WIDTH]) for j in range(6)]
    k_cmp, v_cmp, k_sel, v_sel, k_win, v_win = kv
    k_sel = jnp.concatenate([rope(k_sel[:, :128]), rope(k_sel[:, 128:])], axis=1)
    k_win = jnp.concatenate([rope(k_win[:, :128]), rope(k_win[:, 128:])], axis=1)
    v_selT = v_sel.T.astype(BF16)
    v_winT = v_win.T.astype(BF16)

    s_start = (pl.program_id(0) % n_row_tiles) * tm
    key_blk = (s_start + lax.broadcasted_iota(jnp.int32, (tm, SEL_LANES), 0)) // SEL_BLOCK
    onehot = (key_blk == lax.broadcasted_iota(jnp.int32, (tm, SEL_LANES), 1)).astype(BF16)
    ones = jnp.ones((HEAD_DIM, tm), BF16)
    for g in range(N_KV_GROUPS):
        sl = slice(g * HEAD_DIM, (g + 1) * HEAD_DIM)
        kc_ref[g] = k_cmp[:, sl]
        vc_ref[g] = v_cmp[:, sl]
        ks_ref[g, :, 0:HEAD_DIM] = k_sel[:, sl].astype(BF16)
        ks_ref[g, :, HEAD_DIM:] = onehot
        kw_ref[g] = k_win[:, sl].astype(BF16)
        vsT_ref[g, 0:HEAD_DIM, :] = v_selT[sl]
        vsT_ref[g, HEAD_DIM:, :] = ones
        vwT_ref[g, 0:HEAD_DIM, :] = v_winT[sl]
        vwT_ref[g, HEAD_DIM:, :] = ones

    gatesT = jax.nn.sigmoid(_dot(xb, wg_ref[...])).T
    for g in range(N_KV_GROUPS):
        gT_ref[g] = gatesT[g * GATES_PER_GROUP:(g + 1) * GATES_PER_GROUP]

    for c in range(wa_ref.shape[1] // 512):
        sl = slice(c * 512, (c + 1) * 512)
        a = _dot(xb, wa_ref[:, sl])
        gt = _dot(xb, wgt_ref[:, sl])
        u_ref[:, sl] = a * jax.nn.sigmoid(gt)


def _in_proj(x2d, wq, wkv, wg, wa, wgt, cos_t, sin_t, *, batch, seq, tm):
    T, d_model = x2d.shape
    n_row_tiles = seq // tm
    G = N_KV_GROUPS
    row = lambda i: (i, 0)
    tab = lambda i: (i % n_row_tiles, 0)
    per_group = lambda i: (i // n_row_tiles, 0, i % n_row_tiles, 0)
    per_group_t = lambda i: (i // n_row_tiles, 0, 0, i % n_row_tiles)

    def grp(width, dtype):
        return (jax.ShapeDtypeStruct((batch, G, seq, width), dtype),
                pl.BlockSpec((None, G, tm, width), per_group))

    def grp_t(height, dtype):
        return (jax.ShapeDtypeStruct((batch, G, height, seq), dtype),
                pl.BlockSpec((None, G, height, tm), per_group_t))

    outs = [
        (jax.ShapeDtypeStruct((wq.shape[1], T), BF16),
         pl.BlockSpec((wq.shape[1], tm), lambda i: (0, i))),
        grp(HEAD_DIM, F32), grp(HEAD_DIM, F32),
        grp(HEAD_DIM + SEL_LANES, BF16), grp_t(2 * HEAD_DIM, BF16),
        grp(HEAD_DIM, BF16), grp_t(2 * HEAD_DIM, BF16),
        grp_t(GATES_PER_GROUP, F32),
        (jax.ShapeDtypeStruct((T, wa.shape[1]), F32), pl.BlockSpec((tm, wa.shape[1]), row)),
    ]
    return pl.pallas_call(
        functools.partial(_in_proj_kernel, tm=tm, n_row_tiles=n_row_tiles),
        grid=(T // tm,),
        in_specs=[pl.BlockSpec((tm, d_model), row),
                  _resident(wq.shape), _resident(wkv.shape), _resident(wg.shape),
                  _resident(wa.shape), _resident(wgt.shape),
                  pl.BlockSpec((tm, 128), tab), pl.BlockSpec((tm, 128), tab)],
        out_specs=[o[1] for o in outs],
        out_shape=[o[0] for o in outs],
        compiler_params=_cparams(1),
        name="in_proj",
    )(x2d, wq, wkv, wg, wa, wgt, cos_t, sin_t)


def _compress_kernel(c_ref, pe_ref, w1a_ref, w1b_ref, w2_ref, w2s_ref, cos_ref, sin_ref,
                     o_ref, *, rotate):
    c = c_ref[...]
    n_chunks = c.shape[0]
    a = _dot_f32(c + pe_ref[0:1, :], w1a_ref[...])
    b = _dot_f32(c + pe_ref[1:2, :], w1b_ref[...])
    hid = jax.nn.gelu(a + pltpu.roll(b, n_chunks - 1, 0))
    out = _dot_f32(hid, w2_ref[...])
    if rotate:
        out = out * cos_ref[...] + _dot_f32(hid, w2s_ref[...]) * sin_ref[...]
        o_ref[...] = out[:, :HEAD_DIM].astype(o_ref.dtype)
    else:
        o_ref[...] = out.T[:HEAD_DIM].astype(o_ref.dtype)


def _compress(chunks, pe2, w1a, w1b, w2, w2s, cos_c, sin_c, *, rotate):
    BG, n_chunks, cw = chunks.shape
    out_block = (None, n_chunks, HEAD_DIM) if rotate else (None, HEAD_DIM, n_chunks)
    return pl.pallas_call(
        functools.partial(_compress_kernel, rotate=rotate),
        grid=(BG,),
        in_specs=[pl.BlockSpec((None, n_chunks, cw), lambda i: (i, 0, 0)),
                  _resident(pe2.shape), _resident(w1a.shape), _resident(w1b.shape),
                  _resident(w2.shape), _resident(w2s.shape),
                  _resident(cos_c.shape), _resident(sin_c.shape)],
        out_specs=pl.BlockSpec(out_block, lambda i: (i, 0, 0)),
        out_shape=jax.ShapeDtypeStruct((BG,) + out_block[1:], BF16),
        compiler_params=_cparams(1),
        name="compress_k" if rotate else "compress_v",
    )(chunks, pe2, w1a, w1b, w2, w2s, cos_c, sin_c)


def _heads_to_lanes(qT):
    return jnp.concatenate(
        [qT[h * HEAD_DIM:(h + 1) * HEAD_DIM] for h in range(HEADS_PER_GROUP)], axis=1)


def _cmp_attn_kernel(qT_ref, kc_ref, vcT_ref, ovT_ref, ocT_ref, sbT_ref, *, n_top):
    qblk = CMP_Q_BLOCK
    s0 = pl.program_id(1) * qblk
    cols = HEADS_PER_GROUP * qblk
    kc = kc_ref[...]
    n_cmp = kc.shape[0]
    s = _dot(kc, _heads_to_lanes(qT_ref[...]))
    t = s0 + (lax.broadcasted_iota(jnp.int32, (n_cmp, cols), 1) & (qblk - 1))
    cmp_end = lax.broadcasted_iota(jnp.int32, (n_cmp, cols), 0) * CMP_STRIDE + (CMP_BLOCK - 1)
    valid = cmp_end <= t
    s = jnp.where(valid, s, NEG_INF)
    m = jnp.max(s, axis=0, keepdims=True)
    e = jnp.where(valid, jnp.exp2(s - m), 0.0)
    den = jnp.sum(e, axis=0, keepdims=True)
    p = e * (1.0 / jnp.maximum(den, 1e-30))
    o = _dot(vcT_ref[...], p.astype(BF16))
    for h in range(HEADS_PER_GROUP):
        ocT_ref[h] = o[:, h * qblk:(h + 1) * qblk]

    p_sum = p[:, 0:qblk]
    for h in range(1, HEADS_PER_GROUP):
        p_sum = p_sum + p[:, h * qblk:(h + 1) * qblk]
    imp = _dot_f32(ovT_ref[...], p_sum)
    tq = s0 + lax.broadcasted_iota(jnp.int32, (SEL_LANES, qblk), 1)
    blk = lax.broadcasted_iota(jnp.int32, (SEL_LANES, qblk), 0)
    cur = tq // SEL_BLOCK
    forced = (blk == 0) | (blk == cur) | (blk == cur - 1)
    v = jnp.where(forced, FORCE_SCORE, jnp.where(blk * SEL_BLOCK <= tq, imp, -1.0))

    idx = blk.astype(F32)
    sel = jnp.zeros((SEL_LANES, qblk), jnp.bool_)
    for _ in range(n_top):
        vmax = jnp.max(v, axis=0, keepdims=True)
        first = jnp.min(jnp.where(v == vmax, idx, float(SEL_LANES)), axis=0, keepdims=True)
        pick = idx == first
        sel = sel | pick
        v = jnp.where(pick, -jnp.inf, v)
    sbT_ref[...] = jnp.where(sel, 0.0, MASK_BIAS).astype(BF16)


def _cmp_attn(qT, kc, vcT, overlapT, *, batch, seq, n_top):
    BG, n_cmp, _ = kc.shape
    G = N_KV_GROUPS
    qblk = CMP_Q_BLOCK
    nqb = seq // qblk
    return pl.pallas_call(
        functools.partial(_cmp_attn_kernel, n_top=n_top),
        grid=(BG, nqb),
        in_specs=[pl.BlockSpec((GROUP_WIDTH, qblk),
                               lambda bg, qb: (bg % G, (bg // G) * nqb + qb)),
                  pl.BlockSpec((None, n_cmp, HEAD_DIM), lambda bg, qb: (bg, 0, 0)),
                  pl.BlockSpec((None, HEAD_DIM, n_cmp), lambda bg, qb: (bg, 0, 0)),
                  _resident(overlapT.shape)],
        out_specs=[pl.BlockSpec((None, HEADS_PER_GROUP, HEAD_DIM, qblk),
                                lambda bg, qb: (bg, 0, 0, qb)),
                   pl.BlockSpec((None, SEL_LANES, qblk), lambda bg, qb: (bg, 0, qb))],
        out_shape=[jax.ShapeDtypeStruct((BG, HEADS_PER_GROUP, HEAD_DIM, seq), F32),
                   jax.ShapeDtypeStruct((BG, SEL_LANES, seq), BF16)],
        compiler_params=_cparams(2),
        name="cmp_attn_select",
    )(qT, kc, vcT, overlapT)


def _sel_win_kernel(qT_ref, sbT_ref, ks_ref, vsT_ref, kw_ref, vwT_ref, ocT_ref, gT_ref, o_ref,
                    *, key_tile, seq):
    qblk = SEL_Q_BLOCK
    s0 = pl.program_id(1) * qblk
    cols = HEADS_PER_GROUP * qblk
    qT = qT_ref[...]
    biasT = sbT_ref[...]
    q_stackT = _heads_to_lanes(qT)
    q_augT = jnp.concatenate(
        [jnp.concatenate([qT[h * HEAD_DIM:(h + 1) * HEAD_DIM], biasT], axis=0)
         for h in range(HEADS_PER_GROUP)], axis=1)
    t_lane = s0 + (lax.broadcasted_iota(jnp.int32, (1, cols), 1) & (qblk - 1))

    def tile(kt, carry, masked):
        m, acc = carry
        nominal = kt * key_tile
        start = pl.multiple_of(jnp.minimum(nominal, seq - key_tile), key_tile)
        s = _dot(ks_ref[pl.ds(start, key_tile), :], q_augT)
        if masked:
            kp = nominal + lax.broadcasted_iota(jnp.int32, (key_tile, 1), 0)
            s = jnp.where(kp <= t_lane, s, MASK_BIAS)
        m_new = jnp.maximum(m, jnp.max(s, axis=0, keepdims=True))
        p = jnp.exp2(s - m_new).astype(BF16)
        acc = jnp.exp2(m - m_new) * acc + _dot(vsT_ref[:, pl.ds(start, key_tile)], p)
        return m_new, acc

    def pair(i, carry, masked):
        return tile(2 * i + 1, tile(2 * i, carry, masked), masked)

    n_tiles = (s0 + qblk - 1) // key_tile + 1
    n_pairs = (n_tiles + 1) // 2
    init = (jnp.full((1, cols), NEG_INF, F32), jnp.zeros((2 * HEAD_DIM, cols), F32))
    carry = lax.fori_loop(0, n_pairs - 1, lambda i, c: pair(i, c, False), init)
    _, acc = pair(n_pairs - 1, carry, True)
    o_s = acc[:HEAD_DIM] * (1.0 / acc[HEAD_DIM:HEAD_DIM + 1])

    span = WINDOW + qblk
    wstart = pl.multiple_of(jnp.maximum(s0 - WINDOW, 0), qblk)
    sw = _dot(kw_ref[pl.ds(wstart, span), :], q_stackT)
    rel = t_lane - (wstart + lax.broadcasted_iota(jnp.int32, (span, 1), 0))
    sw = jnp.where((rel >= 0) & (rel < WINDOW), sw, NEG_INF)
    pw = jnp.exp2(sw - jnp.max(sw, axis=0, keepdims=True)).astype(BF16)
    accw = _dot(vwT_ref[:, pl.ds(wstart, span)], pw)
    o_w = accw[:HEAD_DIM] * (1.0 / accw[HEAD_DIM:HEAD_DIM + 1])

    gT = gT_ref[...]
    outs = []
    for h in range(HEADS_PER_GROUP):
        c = slice(h * qblk, (h + 1) * qblk)
        gate = lambda br: gT[h * N_BRANCH + br:h * N_BRANCH + br + 1, :]
        outs.append(gate(0) * ocT_ref[h] + gate(1) * o_s[:, c] + gate(2) * o_w[:, c])
    o_ref[...] = jnp.concatenate(outs, axis=0).T.astype(o_ref.dtype)


def _sel_win_attn(qT, sel_biasT, ks_aug, vsT_aug, kw, vwT_aug, o_cmpT, gatesT, *,
                  batch, seq, key_tile):
    G = N_KV_GROUPS
    BG = batch * G
    qblk = SEL_Q_BLOCK
    nqb = seq // qblk
    whole = lambda a: pl.BlockSpec((None,) + a.shape[1:], lambda bg, qb: (bg, 0, 0))
    return pl.pallas_call(
        functools.partial(_sel_win_kernel, key_tile=key_tile, seq=seq),
        grid=(BG, nqb),
        in_specs=[pl.BlockSpec((GROUP_WIDTH, qblk),
                               lambda bg, qb: (bg % G, (bg // G) * nqb + qb)),
                  pl.BlockSpec((None, SEL_LANES, qblk), lambda bg, qb: (bg, 0, qb)),
                  whole(ks_aug), whole(vsT_aug), whole(kw), whole(vwT_aug),
                  pl.BlockSpec((None, HEADS_PER_GROUP, HEAD_DIM, qblk),
                               lambda bg, qb: (bg, 0, 0, qb)),
                  pl.BlockSpec((None, GATES_PER_GROUP, qblk), lambda bg, qb: (bg, 0, qb))],
        out_specs=pl.BlockSpec((qblk, GROUP_WIDTH),
                               lambda bg, qb: ((bg // G) * nqb + qb, bg % G)),
        out_shape=jax.ShapeDtypeStruct((batch * seq, N_Q_HEADS * HEAD_DIM), BF16),
        compiler_params=_cparams(2),
        name="sel_win_attn",
    )(qT, sel_biasT, ks_aug, vsT_aug, kw, vwT_aug, o_cmpT, gatesT)


def _conv_kernel(u_ref, halo_ref, w_ref, b_ref, g_ref, beta_ref, o_ref, win_ref,
                 *, tm, n_row_tiles):
    first = (pl.program_id(0) % n_row_tiles) == 0
    halo = halo_ref[...]
    win_ref[0:CONV_HALO, :] = jnp.where(first, 0.0, halo)
    win_ref[CONV_HALO:, :] = u_ref[...]
    width = u_ref.shape[1]
    cols = []
    for c in range(width // 128):
        sl = slice(c * 128, (c + 1) * 128)
        acc = jnp.zeros((tm, 128), F32)
        for j in range(CONV_KERNEL):
            off = CONV_HALO - (CONV_KERNEL - 1) + j
            acc = acc + win_ref[off:off + tm, sl] * w_ref[j:j + 1, sl]
        cols.append(acc)
    y = jnp.concatenate(cols, axis=1) + b_ref[...]
    y = _layer_norm(y, g_ref[...], beta_ref[...])
    o_ref[...] = (y * jax.nn.sigmoid(y)).astype(o_ref.dtype)


def _conformer_conv(u, w, b, g, beta, *, seq, tm):
    T, width = u.shape
    n_row_tiles = seq // tm
    halo_blocks = tm // CONV_HALO
    return pl.pallas_call(
        functools.partial(_conv_kernel, tm=tm, n_row_tiles=n_row_tiles),
        grid=(T // tm,),
        in_specs=[pl.BlockSpec((tm, width), lambda i: (i, 0)),
                  pl.BlockSpec((CONV_HALO, width),
                               lambda i: (jnp.maximum(i * halo_blocks - 1, 0), 0)),
                  _resident(w.shape), _resident(b.shape), _resident(g.shape),
                  _resident(beta.shape)],
        out_specs=pl.BlockSpec((tm, width), lambda i: (i, 0)),
        out_shape=jax.ShapeDtypeStruct((T, width), BF16),
        scratch_shapes=[pltpu.VMEM((CONV_HALO + tm, width), F32)],
        compiler_params=_cparams(1),
        name="conformer_conv",
    )(u, u, w, b, g, beta)


def _out_proj_kernel(a_ref, c_ref, x_ref, wa_ref, wc_ref, g_ref, b_ref, o_ref, *, alpha):
    mix = _dot(a_ref[...], wa_ref[...]) + _dot(c_ref[...], wc_ref[...])
    o_ref[...] = _layer_norm(alpha * x_ref[...] + mix, g_ref[...], b_ref[...])


def _out_proj(attn, conv, x2d, w_attn, w_conv, g, b, *, alpha, tm):
    T, d_model = x2d.shape
    row = lambda i: (i, 0)
    return pl.pallas_call(
        functools.partial(_out_proj_kernel, alpha=alpha),
        grid=(T // tm,),
        in_specs=[pl.BlockSpec((tm, attn.shape[1]), row), pl.BlockSpec((tm, conv.shape[1]), row),
                  pl.BlockSpec((tm, d_model), row),
                  _resident(w_attn.shape), _resident(w_conv.shape),
                  _resident(g.shape), _resident(b.shape)],
        out_specs=pl.BlockSpec((tm, d_model), row),
        out_shape=jax.ShapeDtypeStruct((T, d_model), F32),
        compiler_params=_cparams(1),
        name="out_proj_deepnorm",
    )(attn, conv, x2d, w_attn, w_conv, g, b)


def _ffn_kernel(h_ref, w1_ref, w2_ref, g_ref, b_ref, o_ref, hb_ref, *, alpha):
    j = pl.program_id(1)

    @pl.when(j == 0)
    def _():
        hb_ref[...] = h_ref[...].astype(BF16)
        o_ref[...] = jnp.zeros_like(o_ref)

    a = jnp.maximum(_dot(hb_ref[...], w1_ref[...]), 0.0)
    o_ref[...] += _dot((a * a).astype(BF16), w2_ref[...])

    @pl.when(j == pl.num_programs(1) - 1)
    def _():
        o_ref[...] = _layer_norm(alpha * h_ref[...] + o_ref[...], g_ref[...], b_ref[...])


def _ffn(h, w1, w2, g, b, *, alpha, tm, tf):
    T, d_model = h.shape
    d_ff = w1.shape[1]
    return pl.pallas_call(
        functools.partial(_ffn_kernel, alpha=alpha),
        grid=(T // tm, d_ff // tf),
        in_specs=[pl.BlockSpec((tm, d_model), lambda i, j: (i, 0)),
                  pl.BlockSpec((d_model, tf), lambda i, j: (0, j)),
                  pl.BlockSpec((tf, d_model), lambda i, j: (j, 0)),
                  _resident(g.shape), _resident(b.shape)],
        out_specs=pl.BlockSpec((tm, d_model), lambda i, j: (i, 0)),
        out_shape=jax.ShapeDtypeStruct((T, d_model), F32),
        scratch_shapes=[pltpu.VMEM((tm, d_model), BF16)],
        compiler_params=pltpu.CompilerParams(dimension_semantics=("parallel", "arbitrary"),
                                             vmem_limit_bytes=VMEM_LIMIT_BYTES),
        name="ffn_deepnorm",
    )(h, w1, w2, g, b)


def _rope_tables(pos, reps):
    half = HEAD_DIM // 2
    inv_freq = ROPE_THETA ** (-jnp.arange(half, dtype=F32) / half)
    ang = pos.astype(F32)[:, None] * inv_freq[None, :]
    cos, sin = jnp.cos(ang), jnp.sin(ang)
    return jnp.tile(jnp.concatenate([cos, cos], axis=1), (1, reps)), \
        jnp.tile(jnp.concatenate([-sin, sin], axis=1), (1, reps))


def _tiles(seq):
    return dict(in_proj=min(512, seq), conv=min(128, seq), out_proj=min(512, seq),
                ffn_rows=min(512, seq), ffn_cols=512, key_tile=min(512, seq))


def _mixer(x2d, w_in, cmp_pe_k, cmp_w1_k, cmp_w2_k, cmp_pe_v, cmp_w1_v, cmp_w2_v,
           conv_w, conv_b, conv_ln_g, conv_ln_b, *, batch, seq):
    tiles = _tiles(seq)
    attn_w = N_Q_HEADS * HEAD_DIM
    kv_end = attn_w + 6 * KV_WIDTH
    gate_end = kv_end + N_Q_HEADS * N_BRANCH
    conv_width = (w_in.shape[1] - gate_end) // 2
    wq = w_in[:, :attn_w].astype(BF16)
    wkv = w_in[:, attn_w:kv_end].astype(BF16)
    wg = jnp.pad(w_in[:, kv_end:gate_end], ((0, 0), (0, 128 - (gate_end - kv_end)))).astype(BF16)
    wa = w_in[:, gate_end:gate_end + conv_width].astype(BF16)
    wgt = w_in[:, gate_end + conv_width:].astype(BF16)
    cos_t, sin_t = _rope_tables(jnp.arange(seq), 2)

    qT, k_cmp, v_cmp, ks_aug, vsT_aug, k_win, vwT_aug, gatesT, u = _in_proj(
        x2d, wq, wkv, wg, wa, wgt, cos_t, sin_t, batch=batch, seq=seq, tm=tiles["in_proj"])

    BG = batch * N_KV_GROUPS
    n_chunks = seq // CMP_STRIDE
    chunk_w = CMP_STRIDE * HEAD_DIM
    cmp_end = jnp.arange(n_chunks) * CMP_STRIDE + CMP_BLOCK - 1
    cos_c, sin_c = _rope_tables(cmp_end, 2)
    pad_cols = lambda w2: jnp.pad(w2, ((0, 0), (0, 128 - HEAD_DIM)))
    swap = lambda w2: jnp.concatenate([w2[:, HEAD_DIM // 2:], w2[:, :HEAD_DIM // 2]], axis=1)

    def compress(t, pe, w1, w2, rotate):
        return _compress(t.reshape(BG, n_chunks, chunk_w), pe.reshape(2, chunk_w),
                         w1[:CMP_STRIDE].reshape(chunk_w, -1), w1[CMP_STRIDE:].reshape(chunk_w, -1),
                         pad_cols(w2), pad_cols(swap(w2)), cos_c, sin_c, rotate=rotate)

    kc = compress(k_cmp, cmp_pe_k, cmp_w1_k, cmp_w2_k, True)
    vcT = compress(v_cmp, cmp_pe_v, cmp_w1_v, cmp_w2_v, False)

    n_slc = seq // SEL_BLOCK
    cmp_start = jnp.arange(n_chunks) * CMP_STRIDE
    slc_start = jnp.arange(SEL_LANES) * SEL_BLOCK
    overlapT = ((cmp_start[None, :] < slc_start[:, None] + SEL_BLOCK)
                & (cmp_start[None, :] + CMP_BLOCK - 1 >= slc_start[:, None])
                & (jnp.arange(n_chunks)[None, :] < n_chunks - 1)
                & (jnp.arange(SEL_LANES)[:, None] < n_slc)).astype(F32)
    o_cmpT, sel_biasT = _cmp_attn(qT, kc, vcT, overlapT, batch=batch, seq=seq,
                                  n_top=min(N_SEL, n_slc))

    flat = lambda a: a.reshape((BG,) + a.shape[2:])
    attn = _sel_win_attn(qT, sel_biasT, flat(ks_aug), flat(vsT_aug), flat(k_win), flat(vwT_aug),
                         o_cmpT, flat(gatesT), batch=batch, seq=seq, key_tile=tiles["key_tile"])

    conv = _conformer_conv(u, conv_w.reshape(CONV_KERNEL, conv_width), conv_b.reshape(1, -1),
                           conv_ln_g.reshape(1, -1), conv_ln_b.reshape(1, -1),
                           seq=seq, tm=tiles["conv"])
    return attn, conv


def kernel(x, w_in, cmp_pe_k, cmp_w1_k, cmp_w2_k, cmp_pe_v, cmp_w1_v, cmp_w2_v, conv_w, conv_b,
           conv_ln_g, conv_ln_b, w_out, ln1_g, ln1_b, w_ff1, w_ff2, ln2_g, ln2_b):
    batch, seq, d_model = x.shape
    depth = w_in.shape[0]
    alpha = (2.0 * depth) ** 0.25
    tiles = _tiles(seq)
    attn_w = N_Q_HEADS * HEAD_DIM
    h = x.reshape(batch * seq, d_model)
    for l in range(depth):
        attn, conv = _mixer(h, w_in[l], cmp_pe_k[l], cmp_w1_k[l], cmp_w2_k[l],
                            cmp_pe_v[l], cmp_w1_v[l], cmp_w2_v[l],
                            conv_w[l], conv_b[l], conv_ln_g[l], conv_ln_b[l],
                            batch=batch, seq=seq)
        h = _out_proj(attn, conv, h, w_out[l, :attn_w].astype(BF16), w_out[l, attn_w:].astype(BF16),
                      ln1_g[l].reshape(1, -1), ln1_b[l].reshape(1, -1),
                      alpha=alpha, tm=tiles["out_proj"])
        h = _ffn(h, w_ff1[l].astype(BF16), w_ff2[l].astype(BF16),
                 ln2_g[l].reshape(1, -1), ln2_b[l].reshape(1, -1),
                 alpha=alpha, tm=tiles["ffn_rows"], tf=tiles["ffn_cols"])
    return h.reshape(batch, seq, d_model)
```

```python
import functools
import math

import jax
import jax.numpy as jnp
from jax import lax
from jax.experimental import pallas as pl
from jax.experimental.pallas import tpu as pltpu

F32 = jnp.float32
BF16 = jnp.bfloat16

HEAD_DIM = 64
N_KV_GROUPS = 4
HEADS_PER_GROUP = 4
N_Q_HEADS = N_KV_GROUPS * HEADS_PER_GROUP
GROUP_WIDTH = HEADS_PER_GROUP * HEAD_DIM
N_BRANCH = 3
KV_WIDTH = N_KV_GROUPS * HEAD_DIM
CMP_BLOCK = 32
CMP_STRIDE = 16
SEL_BLOCK = 64
N_SEL = 16
WINDOW = 512
NEG_INF = -1.0e30
MASK_BIAS = -(2.0 ** 100)
CONV_KERNEL = 31
CONV_HALO = 32
ROPE_THETA = 10000.0
LN_EPS = 1e-5
SEL_LANES = 128
CMP_Q_BLOCK = 256
SEL_Q_BLOCK = 256
GATES_PER_GROUP = HEADS_PER_GROUP * N_BRANCH
VMEM_LIMIT_BYTES = 56 * 1024 * 1024


def _cparams(n_axes):
    return pltpu.CompilerParams(dimension_semantics=("parallel",) * n_axes,
                                vmem_limit_bytes=VMEM_LIMIT_BYTES)


def _resident(shape):
    nd = len(shape)
    return pl.BlockSpec(shape, lambda *_: (0,) * nd, pipeline_mode=pl.Buffered(1))


def _dot(a, b):
    return jnp.dot(a, b, preferred_element_type=F32)


def _dot_f32(a, b):
    return jnp.dot(a, b, preferred_element_type=F32, precision=lax.Precision.HIGHEST)


def _layer_norm(y, g, b):
    mu = jnp.mean(y, axis=-1, keepdims=True)
    d = y - mu
    var = jnp.mean(d * d, axis=-1, keepdims=True)
    return d * lax.rsqrt(var + LN_EPS) * g + b


def _in_proj_kernel(x_ref, wq_ref, wkv_ref, wg_ref, wa_ref, wgt_ref, cos_ref, sin_ref,
                    qT_ref, kc_ref, vc_ref, ks_ref, vsT_ref, kw_ref, vwT_ref, gT_ref, u_ref,
                    *, tm, n_row_tiles):
    xb = x_ref[...].astype(BF16)
    cos = cos_ref[...]
    sin = sin_ref[...]
    lane = lax.broadcasted_iota(jnp.int32, (tm, 128), 1)
    first_half = (lane & (HEAD_DIM - 1)) < (HEAD_DIM // 2)

    def rope(t):
        partner = jnp.where(first_half, pltpu.roll(t, 128 - HEAD_DIM // 2, 1),
                            pltpu.roll(t, HEAD_DIM // 2, 1))
        return t * cos + partner * sin

    q_scale = HEAD_DIM ** -0.5 * math.log2(math.e)
    for c in range(wq_ref.shape[1] // 512):
        t = _dot(xb, wq_ref[:, c * 512:(c + 1) * 512])
        t = jnp.concatenate([rope(t[:, cc * 128:(cc + 1) * 128]) for cc in range(4)], axis=1)
        qT_ref[c * 512:(c + 1) * 512, :] = (t * q_scale).T.astype(BF16)

    kv = [_dot(xb, wkv_ref[:, j * KV_WIDTH:(j + 1) * KV_WIDTH]) for j in range(6)]
    k_cmp, v_cmp, k_sel, v_sel, k_win, v_win = kv
    k_sel = jnp.concatenate([rope(k_sel[:, :128]), rope(k_sel[:, 128:])], axis=1)
    k_win = jnp.concatenate([rope(k_win[:, :128]), rope(k_win[:, 128:])], axis=1)
    v_selT = v_sel.T.astype(BF16)
    v_winT = v_win.T.astype(BF16)

    s_start = (pl.program_id(0) % n_row_tiles) * tm
    key_blk = (s_start + lax.broadcasted_iota(jnp.int32, (tm, SEL_LANES), 0)) // SEL_BLOCK
    onehot = (key_blk == lax.broadcasted_iota(jnp.int32, (tm, SEL_LANES), 1)).astype(BF16)
    ones = jnp.ones((HEAD_DIM, tm), BF16)
    for g in range(N_KV_GROUPS):
        sl = slice(g * HEAD_DIM, (g + 1) * HEAD_DIM)
        kc_ref[g] = k_cmp[:, sl]
        vc_ref[g] = v_cmp[:, sl]
        ks_ref[g, :, 0:HEAD_DIM] = k_sel[:, sl].astype(BF16)
        ks_ref[g, :, HEAD_DIM:] = onehot
        kw_ref[g] = k_win[:, sl].astype(BF16)
        vsT_ref[g, 0:HEAD_DIM, :] = v_selT[sl]
        vsT_ref[g, HEAD_DIM:, :] = ones
        vwT_ref[g, 0:HEAD_DIM, :] = v_winT[sl]
        vwT_ref[g, HEAD_DIM:, :] = ones

    gatesT = jax.nn.sigmoid(_dot(xb, wg_ref[...])).T
    for g in range(N_KV_GROUPS):
        gT_ref[g] = gatesT[g * GATES_PER_GROUP:(g + 1) * GATES_PER_GROUP]

    for c in range(wa_ref.shape[1] // 512):
        sl = slice(c * 512, (c + 1) * 512)
        a = _dot(xb, wa_ref[:, sl])
        gt = _dot(xb, wgt_ref[:, sl])
        u_ref[:, sl] = a * jax.nn.sigmoid(gt)


def _in_proj(x2d, wq, wkv, wg, wa, wgt, cos_t, sin_t, *, batch, seq, tm):
    T, d_model = x2d.shape
    n_row_tiles = seq // tm
    G = N_KV_GROUPS
    row = lambda i: (i, 0)
    tab = lambda i: (i % n_row_tiles, 0)
    per_group = lambda i: (i // n_row_tiles, 0, i % n_row_tiles, 0)
    per_group_t = lambda i: (i // n_row_tiles, 0, 0, i % n_row_tiles)

    def grp(width, dtype):
        return (jax.ShapeDtypeStruct((batch, G, seq, width), dtype),
                pl.BlockSpec((None, G, tm, width), per_group))

    def grp_t(height, dtype):
        return (jax.ShapeDtypeStruct((batch, G, height, seq), dtype),
                pl.BlockSpec((None, G, height, tm), per_group_t))

    outs = [
        (jax.ShapeDtypeStruct((wq.shape[1], T), BF16),
         pl.BlockSpec((wq.shape[1], tm), lambda i: (0, i))),
        grp(HEAD_DIM, F32), grp(HEAD_DIM, F32),
        grp(HEAD_DIM + SEL_LANES, BF16), grp_t(2 * HEAD_DIM, BF16),
        grp(HEAD_DIM, BF16), grp_t(2 * HEAD_DIM, BF16),
        grp_t(GATES_PER_GROUP, F32),
        (jax.ShapeDtypeStruct((T, wa.shape[1]), F32), pl.BlockSpec((tm, wa.shape[1]), row)),
    ]
    return pl.pallas_call(
        functools.partial(_in_proj_kernel, tm=tm, n_row_tiles=n_row_tiles),
        grid=(T // tm,),
        in_specs=[pl.BlockSpec((tm, d_model), row),
                  _resident(wq.shape), _resident(wkv.shape), _resident(wg.shape),
                  _resident(wa.shape), _resident(wgt.shape),
                  pl.BlockSpec((tm, 128), tab), pl.BlockSpec((tm, 128), tab)],
        out_specs=[o[1] for o in outs],
        out_shape=[o[0] for o in outs],
        compiler_params=_cparams(1),
        name="in_proj",
    )(x2d, wq, wkv, wg, wa, wgt, cos_t, sin_t)


def _compress_kernel(c_ref, pe_ref, w1a_ref, w1b_ref, w2_ref, w2s_ref, cos_ref, sin_ref,
                     o_ref, *, rotate):
    c = c_ref[...]
    n_chunks = c.shape[0]
    a = _dot_f32(c + pe_ref[0:1, :], w1a_ref[...])
    b = _dot_f32(c + pe_ref[1:2, :], w1b_ref[...])
    hid = jax.nn.gelu(a + pltpu.roll(b, n_chunks - 1, 0))
    out = _dot_f32(hid, w2_ref[...])
    if rotate:
        out = out * cos_ref[...] + _dot_f32(hid, w2s_ref[...]) * sin_ref[...]
        o_ref[...] = out[:, :HEAD_DIM].astype(o_ref.dtype)
    else:
        o_ref[...] = out.T[:HEAD_DIM].astype(o_ref.dtype)


def _compress(chunks, pe2, w1a, w1b, w2, w2s, cos_c, sin_c, *, rotate):
    BG, n_chunks, cw = chunks.shape
    out_block = (None, n_chunks, HEAD_DIM) if rotate else (None, HEAD_DIM, n_chunks)
    return pl.pallas_call(
        functools.partial(_compress_kernel, rotate=rotate),
        grid=(BG,),
        in_specs=[pl.BlockSpec((None, n_chunks, cw), lambda i: (i, 0, 0)),
                  _resident(pe2.shape), _resident(w1a.shape), _resident(w1b.shape),
                  _resident(w2.shape), _resident(w2s.shape),
                  _resident(cos_c.shape), _resident(sin_c.shape)],
        out_specs=pl.BlockSpec(out_block, lambda i: (i, 0, 0)),
        out_shape=jax.ShapeDtypeStruct((BG,) + out_block[1:], BF16),
        compiler_params=_cparams(1),
        name="compress_k" if rotate else "compress_v",
    )(chunks, pe2, w1a, w1b, w2, w2s, cos_c, sin_c)


def _heads_to_lanes(qT):
    return jnp.concatenate(
        [qT[h * HEAD_DIM:(h + 1) * HEAD_DIM] for h in range(HEADS_PER_GROUP)], axis=1)


def _tile_heads(a):
    return jnp.concatenate([a] * HEADS_PER_GROUP, axis=1)


def _cmp_attn_kernel(qT_ref, kc_ref, vcT_ref, ovT_ref, ocT_ref, sbT_ref, *, n_top):
    qblk = CMP_Q_BLOCK
    s0 = pl.program_id(1) * qblk
    cols = HEADS_PER_GROUP * qblk
    kc = kc_ref[...]
    n_cmp = kc.shape[0]
    base = (lax.broadcasted_iota(jnp.int32, (n_cmp, qblk), 0) * CMP_STRIDE
            - lax.broadcasted_iota(jnp.int32, (n_cmp, qblk), 1))
    bias = jnp.where(base <= s0 - (CMP_BLOCK - 1), 0.0, NEG_INF)
    s = _dot(kc, _heads_to_lanes(qT_ref[...])) + _tile_heads(bias)
    m = jnp.max(s, axis=0, keepdims=True)
    e = jnp.exp2(s - m)
    den = jnp.sum(e, axis=0, keepdims=True)
    t_lane = s0 + (lax.broadcasted_iota(jnp.int32, (1, cols), 1) & (qblk - 1))
    p = e * jnp.where(t_lane >= CMP_BLOCK - 1, 1.0 / den, 0.0)
    o = _dot(vcT_ref[...], p.astype(BF16))
    for h in range(HEADS_PER_GROUP):
        ocT_ref[h] = o[:, h * qblk:(h + 1) * qblk]

    p_sum = p[:, 0:qblk]
    for h in range(1, HEADS_PER_GROUP):
        p_sum = p_sum + p[:, h * qblk:(h + 1) * qblk]
    imp = _dot_f32(ovT_ref[...], p_sum)
    tq = s0 + lax.broadcasted_iota(jnp.int32, (SEL_LANES, qblk), 1)
    blk = lax.broadcasted_iota(jnp.int32, (SEL_LANES, qblk), 0)
    cur = tq // SEL_BLOCK
    forced = (blk == 0) | (blk == cur) | (blk == cur - 1)
    v = jnp.where(forced, -jnp.inf, jnp.where(blk * SEL_BLOCK <= tq, imp, -1.0))

    idx = blk.astype(F32)
    sel = forced
    for _ in range(n_top - 3):
        vmax = jnp.max(v, axis=0, keepdims=True)
        first = jnp.min(jnp.where(v == vmax, idx, float(SEL_LANES)), axis=0, keepdims=True)
        pick = idx == first
        sel = sel | pick
        v = jnp.where(pick, -jnp.inf, v)
    sbT_ref[...] = jnp.where(sel, 0.0, MASK_BIAS).astype(BF16)


def _cmp_attn(qT, kc, vcT, overlapT, *, batch, seq, n_top):
    BG, n_cmp, _ = kc.shape
    G = N_KV_GROUPS
    qblk = CMP_Q_BLOCK
    nqb = seq // qblk
    return pl.pallas_call(
        functools.partial(_cmp_attn_kernel, n_top=n_top),
        grid=(BG, nqb),
        in_specs=[pl.BlockSpec((GROUP_WIDTH, qblk),
                               lambda bg, qb: (bg % G, (bg // G) * nqb + qb)),
                  pl.BlockSpec((None, n_cmp, HEAD_DIM), lambda bg, qb: (bg, 0, 0)),
                  pl.BlockSpec((None, HEAD_DIM, n_cmp), lambda bg, qb: (bg, 0, 0)),
                  _resident(overlapT.shape)],
        out_specs=[pl.BlockSpec((None, HEADS_PER_GROUP, HEAD_DIM, qblk),
                                lambda bg, qb: (bg, 0, 0, qb)),
                   pl.BlockSpec((None, SEL_LANES, qblk), lambda bg, qb: (bg, 0, qb))],
        out_shape=[jax.ShapeDtypeStruct((BG, HEADS_PER_GROUP, HEAD_DIM, seq), F32),
                   jax.ShapeDtypeStruct((BG, SEL_LANES, seq), BF16)],
        compiler_params=_cparams(2),
        name="cmp_attn_select",
    )(qT, kc, vcT, overlapT)


def _sel_win_kernel(qT_ref, sbT_ref, ks_ref, vsT_ref, kw_ref, vwT_ref, ocT_ref, gT_ref, o_ref,
                    s_ref, p_ref, *, key_tile, seq):
    qblk = SEL_Q_BLOCK
    s0 = pl.program_id(1) * qblk
    cols = HEADS_PER_GROUP * qblk
    qT = qT_ref[...]
    biasT = sbT_ref[...]
    q_stackT = _heads_to_lanes(qT)
    q_augT = jnp.concatenate(
        [jnp.concatenate([qT[h * HEAD_DIM:(h + 1) * HEAD_DIM], biasT], axis=0)
         for h in range(HEADS_PER_GROUP)], axis=1)

    def tile_start(kt):
        return pl.multiple_of(jnp.minimum(kt * key_tile, seq - key_tile), key_tile)

    def scores(kt):
        return _dot(ks_ref[pl.ds(tile_start(kt), key_tile), :], q_augT)

    def pv(kt, p):
        return _dot(vsT_ref[:, pl.ds(tile_start(kt), key_tile)], p)

    def softmax_step(s, kt, m, masked):
        if masked:
            d = (lax.broadcasted_iota(jnp.int32, (key_tile, qblk), 0)
                 - lax.broadcasted_iota(jnp.int32, (key_tile, qblk), 1))
            s = s + _tile_heads(jnp.where(d <= s0 - kt * key_tile, 0.0, MASK_BIAS))
        m_new = jnp.maximum(m, jnp.max(s, axis=0, keepdims=True))
        return m_new, jnp.exp2(m - m_new), jnp.exp2(s - m_new).astype(BF16)

    def pair(i, carry, last):
        m, acc = carry
        pv_prev = pv(jnp.maximum(2 * i - 1, 0), p_ref[...])
        s_odd = scores(2 * i + 1)
        m, alpha, p = softmax_step(s_ref[...], 2 * i, m, last)
        acc = alpha * (acc + pv_prev) + pv(2 * i, p)
        if not last:
            s_ref[...] = scores(2 * i + 2)
        m, alpha, p = softmax_step(s_odd, 2 * i + 1, m, last)
        if last:
            return m, alpha * acc + pv(2 * i + 1, p)
        p_ref[...] = p
        return m, alpha * acc

    s_ref[...] = scores(0)
    p_ref[...] = jnp.zeros_like(p_ref)
    wstart = pl.multiple_of(jnp.maximum(s0 - WINDOW, 0), qblk)
    d = (lax.broadcasted_iota(jnp.int32, (qblk, qblk), 0)
         - lax.broadcasted_iota(jnp.int32, (qblk, qblk), 1))
    sw = []
    for c in range(WINDOW // qblk + 1):
        off = s0 - wstart - c * qblk
        wbias = jnp.where((d <= off) & (d > off - WINDOW), 0.0, NEG_INF)
        sw.append(_dot(kw_ref[pl.ds(wstart + c * qblk, qblk), :], q_stackT) + _tile_heads(wbias))
    sw = jnp.concatenate(sw, axis=0)
    pw = jnp.exp2(sw - jnp.max(sw, axis=0, keepdims=True)).astype(BF16)
    accw = _dot(vwT_ref[:, pl.ds(wstart, WINDOW + qblk)], pw)
    o_w = accw[:HEAD_DIM] * (1.0 / accw[HEAD_DIM:HEAD_DIM + 1])

    n_tiles = (s0 + qblk - 1) // key_tile + 1
    n_pairs = (n_tiles + 1) // 2
    init = (jnp.full((1, cols), NEG_INF, F32), jnp.zeros((2 * HEAD_DIM, cols), F32))
    carry = lax.fori_loop(0, n_pairs - 1, lambda i, c: pair(i, c, False), init)
    _, acc = pair(n_pairs - 1, carry, True)
    o_s = acc[:HEAD_DIM] * (1.0 / acc[HEAD_DIM:HEAD_DIM + 1])

    gT = gT_ref[...]
    outs = []
    for h in range(HEADS_PER_GROUP):
        c = slice(h * qblk, (h + 1) * qblk)
        gate = lambda br: gT[h * N_BRANCH + br:h * N_BRANCH + br + 1, :]
        outs.append(gate(0) * ocT_ref[h] + gate(1) * o_s[:, c] + gate(2) * o_w[:, c])
    o_ref[...] = jnp.concatenate(outs, axis=0).T.astype(o_ref.dtype)


def _sel_win_attn(qT, sel_biasT, ks_aug, vsT_aug, kw, vwT_aug, o_cmpT, gatesT, *,
                  batch, seq, key_tile):
    G = N_KV_GROUPS
    BG = batch * G
    qblk = SEL_Q_BLOCK
    nqb = seq // qblk
    whole = lambda a: pl.BlockSpec((None,) + a.shape[1:], lambda bg, qb: (bg, 0, 0))
    return pl.pallas_call(
        functools.partial(_sel_win_kernel, key_tile=key_tile, seq=seq),
        grid=(BG, nqb),
        in_specs=[pl.BlockSpec((GROUP_WIDTH, qblk),
                               lambda bg, qb: (bg % G, (bg // G) * nqb + qb)),
                  pl.BlockSpec((None, SEL_LANES, qblk), lambda bg, qb: (bg, 0, qb)),
                  whole(ks_aug), whole(vsT_aug), whole(kw), whole(vwT_aug),
                  pl.BlockSpec((None, HEADS_PER_GROUP, HEAD_DIM, qblk),
                               lambda bg, qb: (bg, 0, 0, qb)),
                  pl.BlockSpec((None, GATES_PER_GROUP, qblk), lambda bg, qb: (bg, 0, qb))],
        out_specs=pl.BlockSpec((qblk, GROUP_WIDTH),
                               lambda bg, qb: ((bg // G) * nqb + qb, bg % G)),
        out_shape=jax.ShapeDtypeStruct((batch * seq, N_Q_HEADS * HEAD_DIM), BF16),
        scratch_shapes=[pltpu.VMEM((key_tile, HEADS_PER_GROUP * qblk), F32),
                        pltpu.VMEM((key_tile, HEADS_PER_GROUP * qblk), BF16)],
        compiler_params=_cparams(2),
        name="sel_win_attn",
    )(qT, sel_biasT, ks_aug, vsT_aug, kw, vwT_aug, o_cmpT, gatesT)


def _conv_kernel(u_ref, halo_ref, w_ref, b_ref, g_ref, beta_ref, o_ref, win_ref,
                 *, tm, n_row_tiles):
    first = (pl.program_id(0) % n_row_tiles) == 0
    halo = halo_ref[...]
    win_ref[0:CONV_HALO, :] = jnp.where(first, 0.0, halo)
    win_ref[CONV_HALO:, :] = u_ref[...]
    width = u_ref.shape[1]
    rows = CONV_HALO + tm
    cols = []
    for c in range(width // 128):
        sl = slice(c * 128, (c + 1) * 128)
        win = win_ref[:, sl]
        acc = jnp.zeros((tm, 128), F32)
        for r in range(8):
            shifted = win if r == 0 else pltpu.roll(win, rows - r, 0)
            for j in range(CONV_KERNEL):
                off = CONV_HALO - (CONV_KERNEL - 1) + j
                if off % 8 == r:
                    acc = acc + shifted[off - r:off - r + tm] * w_ref[j:j + 1, sl]
        cols.append(acc)
    y = jnp.concatenate(cols, axis=1) + b_ref[...]
    y = _layer_norm(y, g_ref[...], beta_ref[...])
    o_ref[...] = (y * jax.nn.sigmoid(y)).astype(o_ref.dtype)


def _conformer_conv(u, w, b, g, beta, *, seq, tm):
    T, width = u.shape
    n_row_tiles = seq // tm
    halo_blocks = tm // CONV_HALO
    return pl.pallas_call(
        functools.partial(_conv_kernel, tm=tm, n_row_tiles=n_row_tiles),
        grid=(T // tm,),
        in_specs=[pl.BlockSpec((tm, width), lambda i: (i, 0)),
                  pl.BlockSpec((CONV_HALO, width),
                               lambda i: (jnp.maximum(i * halo_blocks - 1, 0), 0)),
                  _resident(w.shape), _resident(b.shape), _resident(g.shape),
                  _resident(beta.shape)],
        out_specs=pl.BlockSpec((tm, width), lambda i: (i, 0)),
        out_shape=jax.ShapeDtypeStruct((T, width), BF16),
        scratch_shapes=[pltpu.VMEM((CONV_HALO + tm, width), F32)],
        compiler_params=_cparams(1),
        name="conformer_conv",
    )(u, u, w, b, g, beta)


def _out_proj_kernel(a_ref, c_ref, x_ref, wa_ref, wc_ref, g_ref, b_ref, o_ref, *, alpha):
    mix = _dot(a_ref[...], wa_ref[...]) + _dot(c_ref[...], wc_ref[...])
    o_ref[...] = _layer_norm(alpha * x_ref[...] + mix, g_ref[...], b_ref[...])


def _out_proj(attn, conv, x2d, w_attn, w_conv, g, b, *, alpha, tm):
    T, d_model = x2d.shape
    row = lambda i: (i, 0)
    return pl.pallas_call(
        functools.partial(_out_proj_kernel, alpha=alpha),
        grid=(T // tm,),
        in_specs=[pl.BlockSpec((tm, attn.shape[1]), row), pl.BlockSpec((tm, conv.shape[1]), row),
                  pl.BlockSpec((tm, d_model), row),
                  _resident(w_attn.shape), _resident(w_conv.shape),
                  _resident(g.shape), _resident(b.shape)],
        out_specs=pl.BlockSpec((tm, d_model), row),
        out_shape=jax.ShapeDtypeStruct((T, d_model), F32),
        compiler_params=_cparams(1),
        name="out_proj_deepnorm",
    )(attn, conv, x2d, w_attn, w_conv, g, b)


def _ffn_kernel(h_ref, w1_ref, w2_ref, g_ref, b_ref, o_ref, hb_ref, *, alpha):
    j = pl.program_id(1)

    @pl.when(j == 0)
    def _():
        hb_ref[...] = h_ref[...].astype(BF16)
        o_ref[...] = jnp.zeros_like(o_ref)

    a = jnp.maximum(_dot(hb_ref[...], w1_ref[...]), 0.0)
    o_ref[...] += _dot((a * a).astype(BF16), w2_ref[...])

    @pl.when(j == pl.num_programs(1) - 1)
    def _():
        o_ref[...] = _layer_norm(alpha * h_ref[...] + o_ref[...], g_ref[...], b_ref[...])


def _ffn(h, w1, w2, g, b, *, alpha, tm, tf):
    T, d_model = h.shape
    d_ff = w1.shape[1]
    return pl.pallas_call(
        functools.partial(_ffn_kernel, alpha=alpha),
        grid=(T // tm, d_ff // tf),
        in_specs=[pl.BlockSpec((tm, d_model), lambda i, j: (i, 0)),
                  pl.BlockSpec((d_model, tf), lambda i, j: (0, j)),
                  pl.BlockSpec((tf, d_model), lambda i, j: (j, 0)),
                  _resident(g.shape), _resident(b.shape)],
        out_specs=pl.BlockSpec((tm, d_model), lambda i, j: (i, 0)),
        out_shape=jax.ShapeDtypeStruct((T, d_model), F32),
        scratch_shapes=[pltpu.VMEM((tm, d_model), BF16)],
        compiler_params=pltpu.CompilerParams(dimension_semantics=("parallel", "arbitrary"),
                                             vmem_limit_bytes=VMEM_LIMIT_BYTES),
        name="ffn_deepnorm",
    )(h, w1, w2, g, b)


def _rope_tables(pos, reps):
    half = HEAD_DIM // 2
    inv_freq = ROPE_THETA ** (-jnp.arange(half, dtype=F32) / half)
    ang = pos.astype(F32)[:, None] * inv_freq[None, :]
    cos, sin = jnp.cos(ang), jnp.sin(ang)
    return jnp.tile(jnp.concatenate([cos, cos], axis=1), (1, reps)), \
        jnp.tile(jnp.concatenate([-sin, sin], axis=1), (1, reps))


def _tiles(seq):
    return dict(in_proj=min(512, seq), conv=min(128, seq), out_proj=min(512, seq),
                ffn_rows=min(512, seq), ffn_cols=512, key_tile=min(512, seq))


def _mixer(x2d, w_in, cmp_pe_k, cmp_w1_k, cmp_w2_k, cmp_pe_v, cmp_w1_v, cmp_w2_v,
           conv_w, conv_b, conv_ln_g, conv_ln_b, *, batch, seq):
    tiles = _tiles(seq)
    attn_w = N_Q_HEADS * HEAD_DIM
    kv_end = attn_w + 6 * KV_WIDTH
    gate_end = kv_end + N_Q_HEADS * N_BRANCH
    conv_width = (w_in.shape[1] - gate_end) // 2
    wq = w_in[:, :attn_w].astype(BF16)
    wkv = w_in[:, attn_w:kv_end].astype(BF16)
    wg = jnp.pad(w_in[:, kv_end:gate_end], ((0, 0), (0, 128 - (gate_end - kv_end)))).astype(BF16)
    wa = w_in[:, gate_end:gate_end + conv_width].astype(BF16)
    wgt = w_in[:, gate_end + conv_width:].astype(BF16)
    cos_t, sin_t = _rope_tables(jnp.arange(seq), 2)

    qT, k_cmp, v_cmp, ks_aug, vsT_aug, k_win, vwT_aug, gatesT, u = _in_proj(
        x2d, wq, wkv, wg, wa, wgt, cos_t, sin_t, batch=batch, seq=seq, tm=tiles["in_proj"])

    BG = batch * N_KV_GROUPS
    n_chunks = seq // CMP_STRIDE
    chunk_w = CMP_STRIDE * HEAD_DIM
    cmp_end = jnp.arange(n_chunks) * CMP_STRIDE + CMP_BLOCK - 1
    cos_c, sin_c = _rope_tables(cmp_end, 2)
    pad_cols = lambda w2: jnp.pad(w2, ((0, 0), (0, 128 - HEAD_DIM)))
    swap = lambda w2: jnp.concatenate([w2[:, HEAD_DIM // 2:], w2[:, :HEAD_DIM // 2]], axis=1)

    def compress(t, pe, w1, w2, rotate):
        return _compress(t.reshape(BG, n_chunks, chunk_w), pe.reshape(2, chunk_w),
                         w1[:CMP_STRIDE].reshape(chunk_w, -1), w1[CMP_STRIDE:].reshape(chunk_w, -1),
                         pad_cols(w2), pad_cols(swap(w2)), cos_c, sin_c, rotate=rotate)

    kc = compress(k_cmp, cmp_pe_k, cmp_w1_k, cmp_w2_k, True)
    vcT = compress(v_cmp, cmp_pe_v, cmp_w1_v, cmp_w2_v, False)

    n_slc = seq // SEL_BLOCK
    cmp_start = jnp.arange(n_chunks) * CMP_STRIDE
    slc_start = jnp.arange(SEL_LANES) * SEL_BLOCK
    overlapT = ((cmp_start[None, :] < slc_start[:, None] + SEL_BLOCK)
                & (cmp_start[None, :] + CMP_BLOCK - 1 >= slc_start[:, None])
                & (jnp.arange(n_chunks)[None, :] < n_chunks - 1)
                & (jnp.arange(SEL_LANES)[:, None] < n_slc)).astype(F32)
    o_cmpT, sel_biasT = _cmp_attn(qT, kc, vcT, overlapT, batch=batch, seq=seq,
                                  n_top=min(N_SEL, n_slc))

    flat = lambda a: a.reshape((BG,) + a.shape[2:])
    attn = _sel_win_attn(qT, sel_biasT, flat(ks_aug), flat(vsT_aug), flat(k_win), flat(vwT_aug),
                         o_cmpT, flat(gatesT), batch=batch, seq=seq, key_tile=tiles["key_tile"])

    conv = _conformer_conv(u, conv_w.reshape(CONV_KERNEL, conv_width), conv_b.reshape(1, -1),
                           conv_ln_g.reshape(1, -1), conv_ln_b.reshape(1, -1),
                           seq=seq, tm=tiles["conv"])
    return attn, conv


def kernel(x, w_in, cmp_pe_k, cmp_w1_k, cmp_w2_k, cmp_pe_v, cmp_w1_v, cmp_w2_v, conv_w, conv_b,
           conv_ln_g, conv_ln_b, w_out, ln1_g, ln1_b, w_ff1, w_ff2, ln2_g, ln2_b):
    batch, seq, d_model = x.shape
    depth = w_in.shape[0]
    alpha = (2.0 * depth) ** 0.25
    tiles = _tiles(seq)
    attn_w = N_Q_HEADS * HEAD_DIM
    h = x.reshape(batch * seq, d_model)
    for l in range(depth):
        attn, conv = _mixer(h, w_in[l], cmp_pe_k[l], cmp_w1_k[l], cmp_w2_k[l],
                            cmp_pe_v[l], cmp_w1_v[l], cmp_w2_v[l],
                            conv_w[l], conv_b[l], conv_ln_g[l], conv_ln_b[l],
                            batch=batch, seq=seq)
        h = _out_proj(attn, conv, h, w_out[l, :attn_w].astype(BF16), w_out[l, attn_w:].astype(BF16),
                      ln1_g[l].reshape(1, -1), ln1_b[l].reshape(1, -1),
                      alpha=alpha, tm=tiles["out_proj"])
        h = _ffn(h, w_ff1[l].astype(BF16), w_ff2[l].astype(BF16),
                 ln2_g[l].reshape(1, -1), ln2_b[l].reshape(1, -1),
                 alpha=alpha, tm=tiles["ffn_rows"], tf=tiles["ffn_cols"])
    return h.reshape(batch, seq, d_model)
```

```python
import functools
import math

import jax
import jax.numpy as jnp
from jax import lax
from jax.experimental import pallas as pl
from jax.experimental.pallas import tpu as pltpu

F32 = jnp.float32
BF16 = jnp.bfloat16

HEAD_DIM = 64
N_KV_GROUPS = 4
HEADS_PER_GROUP = 4
N_Q_HEADS = N_KV_GROUPS * HEADS_PER_GROUP
GROUP_WIDTH = HEADS_PER_GROUP * HEAD_DIM
N_BRANCH = 3
KV_WIDTH = N_KV_GROUPS * HEAD_DIM
CMP_BLOCK = 32
CMP_STRIDE = 16
SEL_BLOCK = 64
N_SEL = 16
WINDOW = 512
NEG_INF = -1.0e30
MASK_BIAS = -(2.0 ** 100)
CONV_KERNEL = 31
CONV_HALO = 32
CONV_ROWS = 128
ROPE_THETA = 10000.0
LN_EPS = 1e-5
SEL_LANES = 128
CMP_Q_BLOCK = 256
SEL_Q_BLOCK = 512
WIN_Q_BLOCK = 256
CMP_ROW_CHUNK = 128
GATES_PER_GROUP = HEADS_PER_GROUP * N_BRANCH
VMEM_LIMIT_BYTES = 56 * 1024 * 1024


def _cparams(n_axes):
    return pltpu.CompilerParams(dimension_semantics=("parallel",) * n_axes,
                                vmem_limit_bytes=VMEM_LIMIT_BYTES)


def _resident(shape):
    nd = len(shape)
    return pl.BlockSpec(shape, lambda *_: (0,) * nd, pipeline_mode=pl.Buffered(1))


def _dot(a, b):
    return jnp.dot(a, b, preferred_element_type=F32)


def _dot_f32(a, b):
    return jnp.dot(a, b, preferred_element_type=F32, precision=lax.Precision.HIGHEST)


def _layer_norm(y, g, b):
    mu = jnp.mean(y, axis=-1, keepdims=True)
    d = y - mu
    var = jnp.mean(d * d, axis=-1, keepdims=True)
    return d * lax.rsqrt(var + LN_EPS) * g + b


def _conv_ln_swish(win_ref, r0, rows, w_ref, b_ref, g_ref, beta_ref):
    span = CONV_HALO + rows
    cols = []
    for c in range(win_ref.shape[1] // 128):
        sl = slice(c * 128, (c + 1) * 128)
        win = win_ref[r0:r0 + span, sl]
        acc = jnp.zeros((rows, 128), F32)
        for r in range(8):
            shifted = win if r == 0 else pltpu.roll(win, span - r, 0)
            for j in range(CONV_KERNEL):
                off = CONV_HALO - (CONV_KERNEL - 1) + j
                if off % 8 == r:
                    acc = acc + shifted[off - r:off - r + rows] * w_ref[j:j + 1, sl]
        cols.append(acc)
    y = jnp.concatenate(cols, axis=1) + b_ref[...]
    y = _layer_norm(y, g_ref[...], beta_ref[...])
    return y * jax.nn.sigmoid(y)


def _in_proj_kernel(x_ref, wq_ref, wkv_ref, wg_ref, wa_ref, wgt_ref, cos_ref, sin_ref,
                    cw_ref, cb_ref, cg_ref, cbeta_ref,
                    qT_ref, kc_ref, vc_ref, ks_ref, vsT_ref, kw_ref, vwT_ref, gT_ref, conv_ref,
                    win_ref, *, tm, n_row_tiles):
    @pl.when(pl.program_id(0) % n_row_tiles == 0)
    def _():
        win_ref[0:CONV_HALO, :] = jnp.zeros((CONV_HALO, win_ref.shape[1]), F32)

    xb = x_ref[...].astype(BF16)
    cos = cos_ref[...]
    sin = sin_ref[...]
    lane = lax.broadcasted_iota(jnp.int32, (tm, 128), 1)
    first_half = (lane & (HEAD_DIM - 1)) < (HEAD_DIM // 2)

    def rope(t):
        partner = jnp.where(first_half, pltpu.roll(t, 128 - HEAD_DIM // 2, 1),
                            pltpu.roll(t, HEAD_DIM // 2, 1))
        return t * cos + partner * sin

    for c in range(wa_ref.shape[1] // 512):
        sl = slice(c * 512, (c + 1) * 512)
        a = _dot(xb, wa_ref[:, sl])
        gt = _dot(xb, wgt_ref[:, sl])
        win_ref[CONV_HALO:, sl] = a * jax.nn.sigmoid(gt)
    for r0 in range(0, tm, CONV_ROWS):
        conv_ref[r0:r0 + CONV_ROWS, :] = _conv_ln_swish(
            win_ref, r0, CONV_ROWS, cw_ref, cb_ref, cg_ref, cbeta_ref).astype(conv_ref.dtype)
    win_ref[0:CONV_HALO, :] = win_ref[tm:tm + CONV_HALO, :]

    q_scale = HEAD_DIM ** -0.5 * math.log2(math.e)
    for c in range(wq_ref.shape[1] // 512):
        t = _dot(xb, wq_ref[:, c * 512:(c + 1) * 512])
        t = jnp.concatenate([rope(t[:, cc * 128:(cc + 1) * 128]) for cc in range(4)], axis=1)
        qT_ref[c * 512:(c + 1) * 512, :] = (t * q_scale).T.astype(BF16)

    kv = [_dot(xb, wkv_ref[:, j * KV_WIDTH:(j + 1) * KV_WIDTH]) for j in range(6)]
    k_cmp, v_cmp, k_sel, v_sel, k_win, v_win = kv
    k_sel = jnp.concatenate([rope(k_sel[:, :128]), rope(k_sel[:, 128:])], axis=1)
    k_win = jnp.concatenate([rope(k_win[:, :128]), rope(k_win[:, 128:])], axis=1)
    v_selT = v_sel.T.astype(BF16)
    v_winT = v_win.T.astype(BF16)

    s_start = (pl.program_id(0) % n_row_tiles) * tm
    key_blk = (s_start + lax.broadcasted_iota(jnp.int32, (tm, SEL_LANES), 0)) // SEL_BLOCK
    onehot = (key_blk == lax.broadcasted_iota(jnp.int32, (tm, SEL_LANES), 1)).astype(BF16)
    ones = jnp.ones((HEAD_DIM, tm), BF16)
    for g in range(N_KV_GROUPS):
        sl = slice(g * HEAD_DIM, (g + 1) * HEAD_DIM)
        kc_ref[g] = k_cmp[:, sl]
        vc_ref[g] = v_cmp[:, sl]
        ks_ref[g, :, 0:HEAD_DIM] = k_sel[:, sl].astype(BF16)
        ks_ref[g, :, HEAD_DIM:] = onehot
        kw_ref[g] = k_win[:, sl].astype(BF16)
        vsT_ref[g, 0:HEAD_DIM, :] = v_selT[sl]
        vsT_ref[g, HEAD_DIM:, :] = ones
        vwT_ref[g, 0:HEAD_DIM, :] = v_winT[sl]
        vwT_ref[g, HEAD_DIM:, :] = ones

    gatesT = jax.nn.sigmoid(_dot(xb, wg_ref[...])).T
    for g in range(N_KV_GROUPS):
        gT_ref[g] = gatesT[g * GATES_PER_GROUP:(g + 1) * GATES_PER_GROUP]


def _in_proj(x2d, wq, wkv, wg, wa, wgt, cos_t, sin_t, conv_w, conv_b, conv_g, conv_beta, *,
             batch, seq, tm):
    T, d_model = x2d.shape
    n_row_tiles = seq // tm
    G = N_KV_GROUPS
    row = lambda i: (i, 0)
    tab = lambda i: (i % n_row_tiles, 0)
    per_group = lambda i: (i // n_row_tiles, 0, i % n_row_tiles, 0)
    per_group_t = lambda i: (i // n_row_tiles, 0, 0, i % n_row_tiles)

    def grp(width, dtype):
        return (jax.ShapeDtypeStruct((batch, G, seq, width), dtype),
                pl.BlockSpec((None, G, tm, width), per_group))

    def grp_t(height, dtype):
        return (jax.ShapeDtypeStruct((batch, G, height, seq), dtype),
                pl.BlockSpec((None, G, height, tm), per_group_t))

    outs = [
        (jax.ShapeDtypeStruct((wq.shape[1], T), BF16),
         pl.BlockSpec((wq.shape[1], tm), lambda i: (0, i))),
        grp(HEAD_DIM, F32), grp(HEAD_DIM, F32),
        grp(HEAD_DIM + SEL_LANES, BF16), grp_t(2 * HEAD_DIM, BF16),
        grp(HEAD_DIM, BF16), grp_t(2 * HEAD_DIM, BF16),
        grp_t(GATES_PER_GROUP, F32),
        (jax.ShapeDtypeStruct((T, wa.shape[1]), BF16), pl.BlockSpec((tm, wa.shape[1]), row)),
    ]
    return pl.pallas_call(
        functools.partial(_in_proj_kernel, tm=tm, n_row_tiles=n_row_tiles),
        grid=(T // tm,),
        in_specs=[pl.BlockSpec((tm, d_model), row),
                  _resident(wq.shape), _resident(wkv.shape), _resident(wg.shape),
                  _resident(wa.shape), _resident(wgt.shape),
                  pl.BlockSpec((tm, 128), tab), pl.BlockSpec((tm, 128), tab),
                  _resident(conv_w.shape), _resident(conv_b.shape),
                  _resident(conv_g.shape), _resident(conv_beta.shape)],
        out_specs=[o[1] for o in outs],
        out_shape=[o[0] for o in outs],
        scratch_shapes=[pltpu.VMEM((CONV_HALO + tm, wa.shape[1]), F32)],
        compiler_params=pltpu.CompilerParams(dimension_semantics=("arbitrary",),
                                             vmem_limit_bytes=VMEM_LIMIT_BYTES),
        name="in_proj",
    )(x2d, wq, wkv, wg, wa, wgt, cos_t, sin_t, conv_w, conv_b, conv_g, conv_beta)


def _compress_kernel(c_ref, pe_ref, w1a_ref, w1b_ref, w2_ref, w2s_ref, cos_ref, sin_ref,
                     o_ref, *, rotate):
    c = c_ref[...]
    n_chunks = c.shape[0]
    a = _dot_f32(c + pe_ref[0:1, :], w1a_ref[...])
    b = _dot_f32(c + pe_ref[1:2, :], w1b_ref[...])
    hid = jax.nn.gelu(a + pltpu.roll(b, n_chunks - 1, 0))
    out = _dot_f32(hid, w2_ref[...])
    if rotate:
        out = out * cos_ref[...] + _dot_f32(hid, w2s_ref[...]) * sin_ref[...]
        o_ref[...] = out[:, :HEAD_DIM].astype(o_ref.dtype)
    else:
        o_ref[...] = out.T[:HEAD_DIM].astype(o_ref.dtype)


def _compress(chunks, pe2, w1a, w1b, w2, w2s, cos_c, sin_c, *, rotate):
    BG, n_chunks, cw = chunks.shape
    out_block = (None, n_chunks, HEAD_DIM) if rotate else (None, HEAD_DIM, n_chunks)
    return pl.pallas_call(
        functools.partial(_compress_kernel, rotate=rotate),
        grid=(BG,),
        in_specs=[pl.BlockSpec((None, n_chunks, cw), lambda i: (i, 0, 0)),
                  _resident(pe2.shape), _resident(w1a.shape), _resident(w1b.shape),
                  _resident(w2.shape), _resident(w2s.shape),
                  _resident(cos_c.shape), _resident(sin_c.shape)],
        out_specs=pl.BlockSpec(out_block, lambda i: (i, 0, 0)),
        out_shape=jax.ShapeDtypeStruct((BG,) + out_block[1:], BF16),
        compiler_params=_cparams(1),
        name="compress_k" if rotate else "compress_v",
    )(chunks, pe2, w1a, w1b, w2, w2s, cos_c, sin_c)


def _heads_to_lanes(qT):
    return jnp.concatenate(
        [qT[h * HEAD_DIM:(h + 1) * HEAD_DIM] for h in range(HEADS_PER_GROUP)], axis=1)


def _tile_heads(a):
    return jnp.concatenate([a] * HEADS_PER_GROUP, axis=1)


def _cmp_attn_kernel(qT_ref, kc_ref, vcT_ref, ovT_ref, ocT_ref, sbT_ref, imp_ref, *, n_top):
    qblk = CMP_Q_BLOCK
    qb = pl.program_id(1)
    s0 = qb * qblk
    cols = HEADS_PER_GROUP * qblk
    n_cmp = kc_ref.shape[0]
    q_stackT = _heads_to_lanes(qT_ref[...])
    t_lane = s0 + (lax.broadcasted_iota(jnp.int32, (1, cols), 1) & (qblk - 1))

    def attend(n):
        base = (lax.broadcasted_iota(jnp.int32, (n, qblk), 0) * CMP_STRIDE
                - lax.broadcasted_iota(jnp.int32, (n, qblk), 1))
        bias = jnp.where(base <= s0 - (CMP_BLOCK - 1), 0.0, NEG_INF)
        s = _dot(kc_ref[0:n, :], q_stackT) + _tile_heads(bias)
        m = jnp.max(s, axis=0, keepdims=True)
        e = jnp.exp2(s - m)
        den = jnp.sum(e, axis=0, keepdims=True)
        p = e * jnp.where(t_lane >= CMP_BLOCK - 1, 1.0 / den, 0.0)
        o = _dot(vcT_ref[:, 0:n], p.astype(BF16))
        for h in range(HEADS_PER_GROUP):
            ocT_ref[h] = o[:, h * qblk:(h + 1) * qblk]
        p_sum = p[:, 0:qblk]
        for h in range(1, HEADS_PER_GROUP):
            p_sum = p_sum + p[:, h * qblk:(h + 1) * qblk]
        imp_ref[...] = _dot_f32(ovT_ref[:, 0:n], p_sum)

    chunks_needed = pl.cdiv((qb + 1) * (qblk // CMP_STRIDE), CMP_ROW_CHUNK)
    for k in range(1, n_cmp // CMP_ROW_CHUNK + 1):
        pl.when(chunks_needed == k)(functools.partial(attend, k * CMP_ROW_CHUNK))

    imp = imp_ref[...]
    tq = s0 + lax.broadcasted_iota(jnp.int32, (SEL_LANES, qblk), 1)
    blk = lax.broadcasted_iota(jnp.int32, (SEL_LANES, qblk), 0)
    cur = tq // SEL_BLOCK
    forced = (blk == 0) | (blk == cur) | (blk == cur - 1)
    v = jnp.where(forced, -jnp.inf, jnp.where(blk * SEL_BLOCK <= tq, imp, -1.0))

    idx = blk.astype(F32)
    sel = forced
    for _ in range(n_top - 3):
        vmax = jnp.max(v, axis=0, keepdims=True)
        first = jnp.min(jnp.where(v == vmax, idx, float(SEL_LANES)), axis=0, keepdims=True)
        pick = idx == first
        sel = sel | pick
        v = jnp.where(pick, -jnp.inf, v)
    sbT_ref[...] = jnp.where(sel, 0.0, MASK_BIAS).astype(BF16)


def _cmp_attn(qT, kc, vcT, overlapT, *, batch, seq, n_top):
    BG, n_cmp, _ = kc.shape
    G = N_KV_GROUPS
    qblk = CMP_Q_BLOCK
    nqb = seq // qblk
    return pl.pallas_call(
        functools.partial(_cmp_attn_kernel, n_top=n_top),
        grid=(BG, nqb),
        in_specs=[pl.BlockSpec((GROUP_WIDTH, qblk),
                               lambda bg, qb: (bg % G, (bg // G) * nqb + qb)),
                  pl.BlockSpec((None, n_cmp, HEAD_DIM), lambda bg, qb: (bg, 0, 0)),
                  pl.BlockSpec((None, HEAD_DIM, n_cmp), lambda bg, qb: (bg, 0, 0)),
                  _resident(overlapT.shape)],
        out_specs=[pl.BlockSpec((None, HEADS_PER_GROUP, HEAD_DIM, qblk),
                                lambda bg, qb: (bg, 0, 0, qb)),
                   pl.BlockSpec((None, SEL_LANES, qblk), lambda bg, qb: (bg, 0, qb))],
        out_shape=[jax.ShapeDtypeStruct((BG, HEADS_PER_GROUP, HEAD_DIM, seq), F32),
                   jax.ShapeDtypeStruct((BG, SEL_LANES, seq), BF16)],
        scratch_shapes=[pltpu.VMEM((SEL_LANES, qblk), F32)],
        compiler_params=_cparams(2),
        name="cmp_attn_select",
    )(qT, kc, vcT, overlapT)


def _sel_win_kernel(qT_ref, sbT_ref, ks_ref, vsT_ref, kw_ref, vwT_ref, ocT_ref, gT_ref, o_ref,
                    s_ref, p_ref):
    qblk = SEL_Q_BLOCK
    s0 = pl.program_id(1) * qblk
    cols = HEADS_PER_GROUP * qblk
    qT = qT_ref[...]
    biasT = sbT_ref[...]
    q_augT = jnp.concatenate(
        [jnp.concatenate([qT[h * HEAD_DIM:(h + 1) * HEAD_DIM], biasT], axis=0)
         for h in range(HEADS_PER_GROUP)], axis=1)

    def tile_start(kt):
        return pl.multiple_of(kt * qblk, qblk)

    def scores(kt):
        return _dot(ks_ref[pl.ds(tile_start(kt), qblk), :], q_augT)

    def pv(kt, p):
        return _dot(vsT_ref[:, pl.ds(tile_start(kt), qblk)], p)

    def softmax_step(s, kt, m, masked):
        if masked:
            d = (lax.broadcasted_iota(jnp.int32, (qblk, qblk), 0)
                 - lax.broadcasted_iota(jnp.int32, (qblk, qblk), 1))
            s = s + _tile_heads(jnp.where(d <= s0 - kt * qblk, 0.0, MASK_BIAS))
        m_new = jnp.maximum(m, jnp.max(s, axis=0, keepdims=True))
        return m_new, jnp.exp2(m - m_new), jnp.exp2(s - m_new).astype(BF16)

    def pair(i, carry):
        m, acc = carry
        pv_prev = pv(jnp.maximum(2 * i - 1, 0), p_ref[...])
        s_odd = scores(2 * i + 1)
        m, alpha, p = softmax_step(s_ref[...], 2 * i, m, False)
        acc = alpha * (acc + pv_prev) + pv(2 * i, p)
        s_ref[...] = scores(2 * i + 2)
        m, alpha, p = softmax_step(s_odd, 2 * i + 1, m, False)
        p_ref[...] = p
        return m, alpha * acc

    def tail(kt, n_full, carry):
        m, acc = carry
        pv_prev = pv(jnp.maximum(kt - 1, 0), p_ref[...])
        m, alpha, p = softmax_step(s_ref[...], kt, m, True)
        acc = alpha * (acc + pv_prev) + pv(kt, p)

        def diagonal(c):
            m, acc = c
            m, alpha, p = softmax_step(scores(kt + 1), kt + 1, m, True)
            return m, alpha * acc + pv(kt + 1, p)

        return lax.cond(kt < n_full, diagonal, lambda c: c, (m, acc))

    s_ref[...] = scores(0)
    p_ref[...] = jnp.zeros_like(p_ref)
    wq = WIN_Q_BLOCK
    d = (lax.broadcasted_iota(jnp.int32, (wq, wq), 0)
         - lax.broadcasted_iota(jnp.int32, (wq, wq), 1))

    def window(sub):
        sq = s0 + sub * wq
        q_sub = _heads_to_lanes(qT[:, sub * wq:(sub + 1) * wq])
        wstart = pl.multiple_of(jnp.maximum(sq - WINDOW, 0), wq)
        sw = []
        for c in range(WINDOW // wq + 1):
            off = sq - wstart - c * wq
            wbias = jnp.where((d <= off) & (d > off - WINDOW), 0.0, NEG_INF)
            sw.append(_dot(kw_ref[pl.ds(wstart + c * wq, wq), :], q_sub) + _tile_heads(wbias))
        sw = jnp.concatenate(sw, axis=0)
        pw = jnp.exp2(sw - jnp.max(sw, axis=0, keepdims=True)).astype(BF16)
        accw = _dot(vwT_ref[:, pl.ds(wstart, WINDOW + wq)], pw)
        return accw[:HEAD_DIM] * (1.0 / accw[HEAD_DIM:HEAD_DIM + 1])

    o_w_subs = [window(sub) for sub in range(qblk // wq)]

    n_full = pl.program_id(1)
    n_main = n_full // 2
    init = (jnp.full((1, cols), NEG_INF, F32), jnp.zeros((2 * HEAD_DIM, cols), F32))
    carry = lax.fori_loop(0, n_main, pair, init)
    _, acc = tail(2 * n_main, n_full, carry)
    o_s = acc[:HEAD_DIM] * (1.0 / acc[HEAD_DIM:HEAD_DIM + 1])

    gT = gT_ref[...]
    outs = []
    for h in range(HEADS_PER_GROUP):
        c = slice(h * qblk, (h + 1) * qblk)
        gate = lambda br: gT[h * N_BRANCH + br:h * N_BRANCH + br + 1, :]
        o_w = jnp.concatenate([o[:, h * wq:(h + 1) * wq] for o in o_w_subs], axis=1)
        outs.append(gate(0) * ocT_ref[h] + gate(1) * o_s[:, c] + gate(2) * o_w)
    o_ref[...] = jnp.concatenate(outs, axis=0).T.astype(o_ref.dtype)


def _sel_win_attn(qT, sel_biasT, ks_aug, vsT_aug, kw, vwT_aug, o_cmpT, gatesT, *,
                  batch, seq):
    G = N_KV_GROUPS
    BG = batch * G
    qblk = SEL_Q_BLOCK
    nqb = seq // qblk
    whole = lambda a: pl.BlockSpec((None,) + a.shape[1:], lambda bg, qb: (bg, 0, 0),
                                   pipeline_mode=pl.Buffered(1))
    return pl.pallas_call(
        _sel_win_kernel,
        grid=(BG, nqb),
        in_specs=[pl.BlockSpec((GROUP_WIDTH, qblk),
                               lambda bg, qb: (bg % G, (bg // G) * nqb + qb)),
                  pl.BlockSpec((None, SEL_LANES, qblk), lambda bg, qb: (bg, 0, qb)),
                  whole(ks_aug), whole(vsT_aug), whole(kw), whole(vwT_aug),
                  pl.BlockSpec((None, HEADS_PER_GROUP, HEAD_DIM, qblk),
                               lambda bg, qb: (bg, 0, 0, qb)),
                  pl.BlockSpec((None, GATES_PER_GROUP, qblk), lambda bg, qb: (bg, 0, qb))],
        out_specs=pl.BlockSpec((qblk, GROUP_WIDTH),
                               lambda bg, qb: ((bg // G) * nqb + qb, bg % G)),
        out_shape=jax.ShapeDtypeStruct((batch * seq, N_Q_HEADS * HEAD_DIM), BF16),
        scratch_shapes=[pltpu.VMEM((qblk, HEADS_PER_GROUP * qblk), F32),
                        pltpu.VMEM((qblk, HEADS_PER_GROUP * qblk), BF16)],
        compiler_params=_cparams(2),
        name="sel_win_attn",
    )(qT, sel_biasT, ks_aug, vsT_aug, kw, vwT_aug, o_cmpT, gatesT)


def _out_proj_kernel(a_ref, c_ref, x_ref, wa_ref, wc_ref, g_ref, b_ref, o_ref, *, alpha):
    mix = _dot(a_ref[...], wa_ref[...]) + _dot(c_ref[...], wc_ref[...])
    o_ref[...] = _layer_norm(alpha * x_ref[...] + mix, g_ref[...], b_ref[...])


def _out_proj(attn, conv, x2d, w_attn, w_conv, g, b, *, alpha, tm):
    T, d_model = x2d.shape
    row = lambda i: (i, 0)
    return pl.pallas_call(
        functools.partial(_out_proj_kernel, alpha=alpha),
        grid=(T // tm,),
        in_specs=[pl.BlockSpec((tm, attn.shape[1]), row), pl.BlockSpec((tm, conv.shape[1]), row),
                  pl.BlockSpec((tm, d_model), row),
                  _resident(w_attn.shape), _resident(w_conv.shape),
                  _resident(g.shape), _resident(b.shape)],
        out_specs=pl.BlockSpec((tm, d_model), row),
        out_shape=jax.ShapeDtypeStruct((T, d_model), F32),
        compiler_params=_cparams(1),
        name="out_proj_deepnorm",
    )(attn, conv, x2d, w_attn, w_conv, g, b)


def _ffn_kernel(h_ref, w1_ref, w2_ref, g_ref, b_ref, o_ref, hb_ref, *, alpha):
    j = pl.program_id(1)

    @pl.when(j == 0)
    def _():
        hb_ref[...] = h_ref[...].astype(BF16)
        o_ref[...] = jnp.zeros_like(o_ref)

    a = jnp.maximum(_dot(hb_ref[...], w1_ref[...]), 0.0)
    o_ref[...] += _dot((a * a).astype(BF16), w2_ref[...])

    @pl.when(j == pl.num_programs(1) - 1)
    def _():
        o_ref[...] = _layer_norm(alpha * h_ref[...] + o_ref[...], g_ref[...], b_ref[...])


def _ffn(h, w1, w2, g, b, *, alpha, tm, tf):
    T, d_model = h.shape
    d_ff = w1.shape[1]
    return pl.pallas_call(
        functools.partial(_ffn_kernel, alpha=alpha),
        grid=(T // tm, d_ff // tf),
        in_specs=[pl.BlockSpec((tm, d_model), lambda i, j: (i, 0)),
                  pl.BlockSpec((d_model, tf), lambda i, j: (0, j)),
                  pl.BlockSpec((tf, d_model), lambda i, j: (j, 0)),
                  _resident(g.shape), _resident(b.shape)],
        out_specs=pl.BlockSpec((tm, d_model), lambda i, j: (i, 0)),
        out_shape=jax.ShapeDtypeStruct((T, d_model), F32),
        scratch_shapes=[pltpu.VMEM((tm, d_model), BF16)],
        compiler_params=pltpu.CompilerParams(dimension_semantics=("parallel", "arbitrary"),
                                             vmem_limit_bytes=VMEM_LIMIT_BYTES),
        name="ffn_deepnorm",
    )(h, w1, w2, g, b)


def _rope_tables(pos, reps):
    half = HEAD_DIM // 2
    inv_freq = ROPE_THETA ** (-jnp.arange(half, dtype=F32) / half)
    ang = pos.astype(F32)[:, None] * inv_freq[None, :]
    cos, sin = jnp.cos(ang), jnp.sin(ang)
    return jnp.tile(jnp.concatenate([cos, cos], axis=1), (1, reps)), \
        jnp.tile(jnp.concatenate([-sin, sin], axis=1), (1, reps))


def _tiles(seq):
    return dict(in_proj=min(512, seq), out_proj=min(512, seq),
                ffn_rows=min(512, seq), ffn_cols=512)


def _mixer(x2d, w_in, cmp_pe_k, cmp_w1_k, cmp_w2_k, cmp_pe_v, cmp_w1_v, cmp_w2_v,
           conv_w, conv_b, conv_ln_g, conv_ln_b, *, batch, seq):
    tiles = _tiles(seq)
    attn_w = N_Q_HEADS * HEAD_DIM
    kv_end = attn_w + 6 * KV_WIDTH
    gate_end = kv_end + N_Q_HEADS * N_BRANCH
    conv_width = (w_in.shape[1] - gate_end) // 2
    wq = w_in[:, :attn_w].astype(BF16)
    wkv = w_in[:, attn_w:kv_end].astype(BF16)
    wg = jnp.pad(w_in[:, kv_end:gate_end], ((0, 0), (0, 128 - (gate_end - kv_end)))).astype(BF16)
    wa = w_in[:, gate_end:gate_end + conv_width].astype(BF16)
    wgt = w_in[:, gate_end + conv_width:].astype(BF16)
    cos_t, sin_t = _rope_tables(jnp.arange(seq), 2)

    qT, k_cmp, v_cmp, ks_aug, vsT_aug, k_win, vwT_aug, gatesT, conv = _in_proj(
        x2d, wq, wkv, wg, wa, wgt, cos_t, sin_t,
        conv_w.reshape(CONV_KERNEL, conv_width), conv_b.reshape(1, -1),
        conv_ln_g.reshape(1, -1), conv_ln_b.reshape(1, -1),
        batch=batch, seq=seq, tm=tiles["in_proj"])

    BG = batch * N_KV_GROUPS
    n_chunks = seq // CMP_STRIDE
    chunk_w = CMP_STRIDE * HEAD_DIM
    cmp_end = jnp.arange(n_chunks) * CMP_STRIDE + CMP_BLOCK - 1
    cos_c, sin_c = _rope_tables(cmp_end, 2)
    pad_cols = lambda w2: jnp.pad(w2, ((0, 0), (0, 128 - HEAD_DIM)))
    swap = lambda w2: jnp.concatenate([w2[:, HEAD_DIM // 2:], w2[:, :HEAD_DIM // 2]], axis=1)

    def compress(t, pe, w1, w2, rotate):
        return _compress(t.reshape(BG, n_chunks, chunk_w), pe.reshape(2, chunk_w),
                         w1[:CMP_STRIDE].reshape(chunk_w, -1), w1[CMP_STRIDE:].reshape(chunk_w, -1),
                         pad_cols(w2), pad_cols(swap(w2)), cos_c, sin_c, rotate=rotate)

    kc = compress(k_cmp, cmp_pe_k, cmp_w1_k, cmp_w2_k, True)
    vcT = compress(v_cmp, cmp_pe_v, cmp_w1_v, cmp_w2_v, False)

    n_slc = seq // SEL_BLOCK
    cmp_start = jnp.arange(n_chunks) * CMP_STRIDE
    slc_start = jnp.arange(SEL_LANES) * SEL_BLOCK
    overlapT = ((cmp_start[None, :] < slc_start[:, None] + SEL_BLOCK)
                & (cmp_start[None, :] + CMP_BLOCK - 1 >= slc_start[:, None])
                & (jnp.arange(n_chunks)[None, :] < n_chunks - 1)
                & (jnp.arange(SEL_LANES)[:, None] < n_slc)).astype(F32)
    o_cmpT, sel_biasT = _cmp_attn(qT, kc, vcT, overlapT, batch=batch, seq=seq,
                                  n_top=min(N_SEL, n_slc))

    flat = lambda a: a.reshape((BG,) + a.shape[2:])
    attn = _sel_win_attn(qT, sel_biasT, flat(ks_aug), flat(vsT_aug), flat(k_win), flat(vwT_aug),
                         o_cmpT, flat(gatesT), batch=batch, seq=seq)
    return attn, conv


def kernel(x, w_in, cmp_pe_k, cmp_w1_k, cmp_w2_k, cmp_pe_v, cmp_w1_v, cmp_w2_v, conv_w, conv_b,
           conv_ln_g, conv_ln_b, w_out, ln1_g, ln1_b, w_ff1, w_ff2, ln2_g, ln2_b):
    batch, seq, d_model = x.shape
    depth = w_in.shape[0]
    alpha = (2.0 * depth) ** 0.25
    tiles = _tiles(seq)
    attn_w = N_Q_HEADS * HEAD_DIM
    h = x.reshape(batch * seq, d_model)
    for l in range(depth):
        attn, conv = _mixer(h, w_in[l], cmp_pe_k[l], cmp_w1_k[l], cmp_w2_k[l],
                            cmp_pe_v[l], cmp_w1_v[l], cmp_w2_v[l],
                            conv_w[l], conv_b[l], conv_ln_g[l], conv_ln_b[l],
                            batch=batch, seq=seq)
        h = _out_proj(attn, conv, h, w_out[l, :attn_w].astype(BF16), w_out[l, attn_w:].astype(BF16),
                      ln1_g[l].reshape(1, -1), ln1_b[l].reshape(1, -1),
                      alpha=alpha, tm=tiles["out_proj"])
        h = _ffn(h, w_ff1[l].astype(BF16), w_ff2[l].astype(BF16),
                 ln2_g[l].reshape(1, -1), ln2_b[l].reshape(1, -1),
                 alpha=alpha, tm=tiles["ffn_rows"], tf=tiles["ffn_cols"])
    return h.reshape(batch, seq, d_model)
```

```python
import functools
import math

import jax
import jax.numpy as jnp
from jax import lax
from jax.experimental import pallas as pl
from jax.experimental.pallas import tpu as pltpu

F32 = jnp.float32
BF16 = jnp.bfloat16

HEAD_DIM = 64
N_KV_GROUPS = 4
HEADS_PER_GROUP = 4
N_Q_HEADS = N_KV_GROUPS * HEADS_PER_GROUP
GROUP_WIDTH = HEADS_PER_GROUP * HEAD_DIM
N_BRANCH = 3
KV_WIDTH = N_KV_GROUPS * HEAD_DIM
CMP_BLOCK = 32
CMP_STRIDE = 16
SEL_BLOCK = 64
N_SEL = 16
WINDOW = 512
NEG_INF = -1.0e30
MASK_BIAS = -(2.0 ** 100)
CONV_KERNEL = 31
CONV_HALO = 32
CONV_ROWS = 128
ROPE_THETA = 10000.0
LN_EPS = 1e-5
SEL_LANES = 128
CMP_Q_BLOCK = 256
SEL_Q_BLOCK = 512
WIN_Q_BLOCK = 256
CMP_ROW_CHUNK = 128
GATES_PER_GROUP = HEADS_PER_GROUP * N_BRANCH
VMEM_LIMIT_BYTES = 56 * 1024 * 1024


def _cparams(n_axes):
    return pltpu.CompilerParams(dimension_semantics=("parallel",) * n_axes,
                                vmem_limit_bytes=VMEM_LIMIT_BYTES)


def _resident(shape):
    nd = len(shape)
    return pl.BlockSpec(shape, lambda *_: (0,) * nd, pipeline_mode=pl.Buffered(1))


def _dot(a, b):
    return jnp.dot(a, b, preferred_element_type=F32)


def _dot_f32(a, b):
    return jnp.dot(a, b, preferred_element_type=F32, precision=lax.Precision.HIGHEST)


def _layer_norm(y, g, b):
    mu = jnp.mean(y, axis=-1, keepdims=True)
    d = y - mu
    var = jnp.mean(d * d, axis=-1, keepdims=True)
    return d * lax.rsqrt(var + LN_EPS) * g + b


def _conv_ln_swish(win_ref, r0, rows, w_ref, b_ref, g_ref, beta_ref):
    span = CONV_HALO + rows
    cols = []
    for c in range(win_ref.shape[1] // 128):
        sl = slice(c * 128, (c + 1) * 128)
        win = win_ref[r0:r0 + span, sl]
        acc = jnp.zeros((rows, 128), F32)
        for r in range(8):
            shifted = win if r == 0 else pltpu.roll(win, span - r, 0)
            for j in range(CONV_KERNEL):
                off = CONV_HALO - (CONV_KERNEL - 1) + j
                if off % 8 == r:
                    acc = acc + shifted[off - r:off - r + rows] * w_ref[j:j + 1, sl]
        cols.append(acc)
    y = jnp.concatenate(cols, axis=1) + b_ref[...]
    y = _layer_norm(y, g_ref[...], beta_ref[...])
    return y * jax.nn.sigmoid(y)


def _in_proj_kernel(x_ref, wq_ref, wkv_ref, wg_ref, wa_ref, wgt_ref, cos_ref, sin_ref,
                    cw_ref, cb_ref, cg_ref, cbeta_ref,
                    qT_ref, kc_ref, vc_ref, ks_ref, vsT_ref, kw_ref, vwT_ref, gT_ref, conv_ref,
                    win_ref, *, tm, n_row_tiles):
    @pl.when(pl.program_id(0) % n_row_tiles == 0)
    def _():
        win_ref[0:CONV_HALO, :] = jnp.zeros((CONV_HALO, win_ref.shape[1]), F32)

    xb = x_ref[...].astype(BF16)
    cos = cos_ref[...]
    sin = sin_ref[...]
    lane = lax.broadcasted_iota(jnp.int32, (tm, 128), 1)
    first_half = (lane & (HEAD_DIM - 1)) < (HEAD_DIM // 2)

    def rope(t):
        partner = jnp.where(first_half, pltpu.roll(t, 128 - HEAD_DIM // 2, 1),
                            pltpu.roll(t, HEAD_DIM // 2, 1))
        return t * cos + partner * sin

    for c in range(wa_ref.shape[1] // 512):
        sl = slice(c * 512, (c + 1) * 512)
        a = _dot(xb, wa_ref[:, sl])
        gt = _dot(xb, wgt_ref[:, sl])
        win_ref[CONV_HALO:, sl] = a * jax.nn.sigmoid(gt)
    for r0 in range(0, tm, CONV_ROWS):
        conv_ref[r0:r0 + CONV_ROWS, :] = _conv_ln_swish(
            win_ref, r0, CONV_ROWS, cw_ref, cb_ref, cg_ref, cbeta_ref).astype(conv_ref.dtype)
    win_ref[0:CONV_HALO, :] = win_ref[tm:tm + CONV_HALO, :]

    q_scale = HEAD_DIM ** -0.5 * math.log2(math.e)
    for c in range(wq_ref.shape[1] // 512):
        t = _dot(xb, wq_ref[:, c * 512:(c + 1) * 512])
        t = jnp.concatenate([rope(t[:, cc * 128:(cc + 1) * 128]) for cc in range(4)], axis=1)
        qT_ref[c * 512:(c + 1) * 512, :] = (t * q_scale).T.astype(BF16)

    kv = [_dot(xb, wkv_ref[:, j * KV_WIDTH:(j + 1) * KV_WIDTH]) for j in range(6)]
    k_cmp, v_cmp, k_sel, v_sel, k_win, v_win = kv
    k_sel = jnp.concatenate([rope(k_sel[:, :128]), rope(k_sel[:, 128:])], axis=1)
    k_win = jnp.concatenate([rope(k_win[:, :128]), rope(k_win[:, 128:])], axis=1)
    v_selT = v_sel.T.astype(BF16)
    v_winT = v_win.T.astype(BF16)

    s_start = (pl.program_id(0) % n_row_tiles) * tm
    key_blk = (s_start + lax.broadcasted_iota(jnp.int32, (tm, SEL_LANES), 0)) // SEL_BLOCK
    onehot = (key_blk == lax.broadcasted_iota(jnp.int32, (tm, SEL_LANES), 1)).astype(BF16)
    ones = jnp.ones((HEAD_DIM, tm), BF16)
    for g in range(N_KV_GROUPS):
        sl = slice(g * HEAD_DIM, (g + 1) * HEAD_DIM)
        kc_ref[g] = k_cmp[:, sl]
        vc_ref[g] = v_cmp[:, sl]
        ks_ref[g, :, 0:HEAD_DIM] = k_sel[:, sl].astype(BF16)
        ks_ref[g, :, HEAD_DIM:] = onehot
        kw_ref[g] = k_win[:, sl].astype(BF16)
        vsT_ref[g, 0:HEAD_DIM, :] = v_selT[sl]
        vsT_ref[g, HEAD_DIM:, :] = ones
        vwT_ref[g, 0:HEAD_DIM, :] = v_winT[sl]
        vwT_ref[g, HEAD_DIM:, :] = ones

    gatesT = jax.nn.sigmoid(_dot(xb, wg_ref[...])).T
    for g in range(N_KV_GROUPS):
        gT_ref[g] = gatesT[g * GATES_PER_GROUP:(g + 1) * GATES_PER_GROUP]


def _in_proj(x2d, wq, wkv, wg, wa, wgt, cos_t, sin_t, conv_w, conv_b, conv_g, conv_beta, *,
             batch, seq, tm):
    T, d_model = x2d.shape
    n_row_tiles = seq // tm
    G = N_KV_GROUPS
    row = lambda i: (i, 0)
    tab = lambda i: (i % n_row_tiles, 0)
    per_group = lambda i: (i // n_row_tiles, 0, i % n_row_tiles, 0)
    per_group_t = lambda i: (i // n_row_tiles, 0, 0, i % n_row_tiles)

    def grp(width, dtype):
        return (jax.ShapeDtypeStruct((batch, G, seq, width), dtype),
                pl.BlockSpec((None, G, tm, width), per_group))

    def grp_t(height, dtype):
        return (jax.ShapeDtypeStruct((batch, G, height, seq), dtype),
                pl.BlockSpec((None, G, height, tm), per_group_t))

    outs = [
        (jax.ShapeDtypeStruct((wq.shape[1], T), BF16),
         pl.BlockSpec((wq.shape[1], tm), lambda i: (0, i))),
        grp(HEAD_DIM, F32), grp(HEAD_DIM, F32),
        grp(HEAD_DIM + SEL_LANES, BF16), grp_t(2 * HEAD_DIM, BF16),
        grp(HEAD_DIM, BF16), grp_t(2 * HEAD_DIM, BF16),
        grp_t(GATES_PER_GROUP, F32),
        (jax.ShapeDtypeStruct((T, wa.shape[1]), BF16), pl.BlockSpec((tm, wa.shape[1]), row)),
    ]
    return pl.pallas_call(
        functools.partial(_in_proj_kernel, tm=tm, n_row_tiles=n_row_tiles),
        grid=(T // tm,),
        in_specs=[pl.BlockSpec((tm, d_model), row),
                  _resident(wq.shape), _resident(wkv.shape), _resident(wg.shape),
                  _resident(wa.shape), _resident(wgt.shape),
                  pl.BlockSpec((tm, 128), tab), pl.BlockSpec((tm, 128), tab),
                  _resident(conv_w.shape), _resident(conv_b.shape),
                  _resident(conv_g.shape), _resident(conv_beta.shape)],
        out_specs=[o[1] for o in outs],
        out_shape=[o[0] for o in outs],
        scratch_shapes=[pltpu.VMEM((CONV_HALO + tm, wa.shape[1]), F32)],
        compiler_params=pltpu.CompilerParams(dimension_semantics=("arbitrary",),
                                             vmem_limit_bytes=VMEM_LIMIT_BYTES),
        name="in_proj",
    )(x2d, wq, wkv, wg, wa, wgt, cos_t, sin_t, conv_w, conv_b, conv_g, conv_beta)


def _compress_kernel(c_ref, pe_ref, w1a_ref, w1b_ref, w2_ref, w2s_ref, cos_ref, sin_ref,
                     o_ref, *, rotate):
    n_chunks = c_ref.shape[0] // CMP_STRIDE
    c = jnp.concatenate([c_ref[pl.ds(l, n_chunks, stride=CMP_STRIDE), :]
                         for l in range(CMP_STRIDE)], axis=1)
    a = _dot_f32(c + pe_ref[0:1, :], w1a_ref[...])
    b = _dot_f32(c + pe_ref[1:2, :], w1b_ref[...])
    hid = jax.nn.gelu(a + pltpu.roll(b, n_chunks - 1, 0))
    out = _dot_f32(hid, w2_ref[...])
    if rotate:
        out = out * cos_ref[...] + _dot_f32(hid, w2s_ref[...]) * sin_ref[...]
        o_ref[...] = out[:, :HEAD_DIM].astype(o_ref.dtype)
    else:
        o_ref[...] = out.T[:HEAD_DIM].astype(o_ref.dtype)


def _compress(tokens, pe2, w1a, w1b, w2, w2s, cos_c, sin_c, *, rotate):
    BG, seq, _ = tokens.shape
    n_chunks = seq // CMP_STRIDE
    out_block = (None, n_chunks, HEAD_DIM) if rotate else (None, HEAD_DIM, n_chunks)
    return pl.pallas_call(
        functools.partial(_compress_kernel, rotate=rotate),
        grid=(BG,),
        in_specs=[pl.BlockSpec((None, seq, HEAD_DIM), lambda i: (i, 0, 0)),
                  _resident(pe2.shape), _resident(w1a.shape), _resident(w1b.shape),
                  _resident(w2.shape), _resident(w2s.shape),
                  _resident(cos_c.shape), _resident(sin_c.shape)],
        out_specs=pl.BlockSpec(out_block, lambda i: (i, 0, 0)),
        out_shape=jax.ShapeDtypeStruct((BG,) + out_block[1:], BF16),
        compiler_params=_cparams(1),
        name="compress_k" if rotate else "compress_v",
    )(tokens, pe2, w1a, w1b, w2, w2s, cos_c, sin_c)


def _heads_to_lanes(qT):
    return jnp.concatenate(
        [qT[h * HEAD_DIM:(h + 1) * HEAD_DIM] for h in range(HEADS_PER_GROUP)], axis=1)


def _tile_heads(a):
    return jnp.concatenate([a] * HEADS_PER_GROUP, axis=1)


def _cmp_attn_kernel(qT_ref, kc_ref, vcT_ref, ovT_ref, ocT_ref, sbT_ref, *, n_top):
    qblk = CMP_Q_BLOCK
    qb = pl.program_id(1)
    s0 = qb * qblk
    cols = HEADS_PER_GROUP * qblk
    n_cmp = kc_ref.shape[0]
    q_stackT = _heads_to_lanes(qT_ref[...])
    t_lane = s0 + (lax.broadcasted_iota(jnp.int32, (1, cols), 1) & (qblk - 1))

    def attend(n):
        base = (lax.broadcasted_iota(jnp.int32, (n, qblk), 0) * CMP_STRIDE
                - lax.broadcasted_iota(jnp.int32, (n, qblk), 1))
        bias = jnp.where(base <= s0 - (CMP_BLOCK - 1), 0.0, NEG_INF)
        s = _dot(kc_ref[0:n, :], q_stackT) + _tile_heads(bias)
        m = jnp.max(s, axis=0, keepdims=True)
        e = jnp.exp2(s - m)
        den = jnp.sum(e, axis=0, keepdims=True)
        p = e * jnp.where(t_lane >= CMP_BLOCK - 1, 1.0 / den, 0.0)
        o = _dot(vcT_ref[:, 0:n], p.astype(BF16))
        for h in range(HEADS_PER_GROUP):
            ocT_ref[h] = o[:, h * qblk:(h + 1) * qblk]
        p_sum = p[:, 0:qblk]
        for h in range(1, HEADS_PER_GROUP):
            p_sum = p_sum + p[:, h * qblk:(h + 1) * qblk]
        n_sel = n * CMP_STRIDE // SEL_BLOCK
        imp = _dot_f32(ovT_ref[0:n_sel, 0:n], p_sum)

        tq = s0 + lax.broadcasted_iota(jnp.int32, (n_sel, qblk), 1)
        blk = lax.broadcasted_iota(jnp.int32, (n_sel, qblk), 0)
        cur = tq // SEL_BLOCK
        forced = (blk == 0) | (blk == cur) | (blk == cur - 1)
        v = jnp.where(forced, -jnp.inf, jnp.where(blk * SEL_BLOCK <= tq, imp, -1.0))
        idx = blk.astype(F32)
        sel = forced
        for _ in range(n_top - 3):
            vmax = jnp.max(v, axis=0, keepdims=True)
            first = jnp.min(jnp.where(v == vmax, idx, float(SEL_LANES)), axis=0, keepdims=True)
            pick = idx == first
            sel = sel | pick
            v = jnp.where(pick, -jnp.inf, v)
        sbT_ref[0:n_sel, :] = jnp.where(sel, 0.0, MASK_BIAS).astype(BF16)
        if n_sel < SEL_LANES:
            sbT_ref[n_sel:, :] = jnp.full((SEL_LANES - n_sel, qblk), MASK_BIAS, BF16)

    chunks_needed = pl.cdiv((qb + 1) * (qblk // CMP_STRIDE), CMP_ROW_CHUNK)
    for k in range(1, n_cmp // CMP_ROW_CHUNK + 1):
        pl.when(chunks_needed == k)(functools.partial(attend, k * CMP_ROW_CHUNK))


def _cmp_attn(qT, kc, vcT, overlapT, *, batch, seq, n_top):
    BG, n_cmp, _ = kc.shape
    G = N_KV_GROUPS
    qblk = CMP_Q_BLOCK
    nqb = seq // qblk
    return pl.pallas_call(
        functools.partial(_cmp_attn_kernel, n_top=n_top),
        grid=(BG, nqb),
        in_specs=[pl.BlockSpec((GROUP_WIDTH, qblk),
                               lambda bg, qb: (bg % G, (bg // G) * nqb + qb)),
                  pl.BlockSpec((None, n_cmp, HEAD_DIM), lambda bg, qb: (bg, 0, 0)),
                  pl.BlockSpec((None, HEAD_DIM, n_cmp), lambda bg, qb: (bg, 0, 0)),
                  _resident(overlapT.shape)],
        out_specs=[pl.BlockSpec((None, HEADS_PER_GROUP, HEAD_DIM, qblk),
                                lambda bg, qb: (bg, 0, 0, qb)),
                   pl.BlockSpec((None, SEL_LANES, qblk), lambda bg, qb: (bg, 0, qb))],
        out_shape=[jax.ShapeDtypeStruct((BG, HEADS_PER_GROUP, HEAD_DIM, seq), F32),
                   jax.ShapeDtypeStruct((BG, SEL_LANES, seq), BF16)],
        compiler_params=_cparams(2),
        name="cmp_attn_select",
    )(qT, kc, vcT, overlapT)


def _sel_win_kernel(qT_ref, sbT_ref, ks_ref, vsT_ref, kw_ref, vwT_ref, ocT_ref, gT_ref, o_ref,
                    s_ref, p_ref):
    qblk = SEL_Q_BLOCK
    s0 = pl.program_id(1) * qblk
    cols = HEADS_PER_GROUP * qblk
    qT = qT_ref[...]
    biasT = sbT_ref[...]
    q_augT = jnp.concatenate(
        [jnp.concatenate([qT[h * HEAD_DIM:(h + 1) * HEAD_DIM], biasT], axis=0)
         for h in range(HEADS_PER_GROUP)], axis=1)

    def tile_start(kt):
        return pl.multiple_of(kt * qblk, qblk)

    def scores(kt):
        return _dot(ks_ref[pl.ds(tile_start(kt), qblk), :], q_augT)

    def pv(kt, p):
        return _dot(vsT_ref[:, pl.ds(tile_start(kt), qblk)], p)

    def softmax_step(s, kt, m, masked):
        if masked:
            d = (lax.broadcasted_iota(jnp.int32, (qblk, qblk), 0)
                 - lax.broadcasted_iota(jnp.int32, (qblk, qblk), 1))
            s = s + _tile_heads(jnp.where(d <= s0 - kt * qblk, 0.0, MASK_BIAS))
        m_new = jnp.maximum(m, jnp.max(s, axis=0, keepdims=True))
        return m_new, jnp.exp2(m - m_new), jnp.exp2(s - m_new).astype(BF16)

    def pair(i, carry):
        m, acc = carry
        pv_prev = pv(jnp.maximum(2 * i - 1, 0), p_ref[...])
        s_odd = scores(2 * i + 1)
        m, alpha, p = softmax_step(s_ref[...], 2 * i, m, False)
        acc = alpha * (acc + pv_prev) + pv(2 * i, p)
        s_ref[...] = scores(2 * i + 2)
        m, alpha, p = softmax_step(s_odd, 2 * i + 1, m, False)
        p_ref[...] = p
        return m, alpha * acc

    def tail(kt, n_full, carry):
        m, acc = carry
        pv_prev = pv(jnp.maximum(kt - 1, 0), p_ref[...])
        m, alpha, p = softmax_step(s_ref[...], kt, m, True)
        acc = alpha * (acc + pv_prev) + pv(kt, p)

        def diagonal(c):
            m, acc = c
            m, alpha, p = softmax_step(scores(kt + 1), kt + 1, m, True)
            return m, alpha * acc + pv(kt + 1, p)

        return lax.cond(kt < n_full, diagonal, lambda c: c, (m, acc))

    s_ref[...] = scores(0)
    p_ref[...] = jnp.zeros_like(p_ref)
    wq = WIN_Q_BLOCK
    d = (lax.broadcasted_iota(jnp.int32, (wq, wq), 0)
         - lax.broadcasted_iota(jnp.int32, (wq, wq), 1))

    def window(sub):
        sq = s0 + sub * wq
        q_sub = _heads_to_lanes(qT[:, sub * wq:(sub + 1) * wq])
        wstart = pl.multiple_of(jnp.maximum(sq - WINDOW, 0), wq)
        sw = []
        for c in range(WINDOW // wq + 1):
            off = sq - wstart - c * wq
            wbias = jnp.where((d <= off) & (d > off - WINDOW), 0.0, NEG_INF)
            sw.append(_dot(kw_ref[pl.ds(wstart + c * wq, wq), :], q_sub) + _tile_heads(wbias))
        sw = jnp.concatenate(sw, axis=0)
        pw = jnp.exp2(sw - jnp.max(sw, axis=0, keepdims=True)).astype(BF16)
        accw = _dot(vwT_ref[:, pl.ds(wstart, WINDOW + wq)], pw)
        return accw[:HEAD_DIM] * (1.0 / accw[HEAD_DIM:HEAD_DIM + 1])

    o_w_subs = [window(sub) for sub in range(qblk // wq)]

    n_full = pl.program_id(1)
    n_main = n_full // 2
    init = (jnp.full((1, cols), NEG_INF, F32), jnp.zeros((2 * HEAD_DIM, cols), F32))
    carry = lax.fori_loop(0, n_main, pair, init)
    _, acc = tail(2 * n_main, n_full, carry)
    o_s = acc[:HEAD_DIM] * (1.0 / acc[HEAD_DIM:HEAD_DIM + 1])

    gT = gT_ref[...]
    outs = []
    for h in range(HEADS_PER_GROUP):
        c = slice(h * qblk, (h + 1) * qblk)
        gate = lambda br: gT[h * N_BRANCH + br:h * N_BRANCH + br + 1, :]
        o_w = jnp.concatenate([o[:, h * wq:(h + 1) * wq] for o in o_w_subs], axis=1)
        outs.append(gate(0) * ocT_ref[h] + gate(1) * o_s[:, c] + gate(2) * o_w)
    o_ref[...] = jnp.concatenate(outs, axis=0).T.astype(o_ref.dtype)


def _sel_win_attn(qT, sel_biasT, ks_aug, vsT_aug, kw, vwT_aug, o_cmpT, gatesT, *,
                  batch, seq):
    G = N_KV_GROUPS
    BG = batch * G
    qblk = SEL_Q_BLOCK
    nqb = seq // qblk
    whole = lambda a: pl.BlockSpec((None,) + a.shape[1:], lambda bg, qb: (bg, 0, 0),
                                   pipeline_mode=pl.Buffered(1))
    return pl.pallas_call(
        _sel_win_kernel,
        grid=(BG, nqb),
        in_specs=[pl.BlockSpec((GROUP_WIDTH, qblk),
                               lambda bg, qb: (bg % G, (bg // G) * nqb + qb)),
                  pl.BlockSpec((None, SEL_LANES, qblk), lambda bg, qb: (bg, 0, qb)),
                  whole(ks_aug), whole(vsT_aug), whole(kw), whole(vwT_aug),
                  pl.BlockSpec((None, HEADS_PER_GROUP, HEAD_DIM, qblk),
                               lambda bg, qb: (bg, 0, 0, qb)),
                  pl.BlockSpec((None, GATES_PER_GROUP, qblk), lambda bg, qb: (bg, 0, qb))],
        out_specs=pl.BlockSpec((qblk, GROUP_WIDTH),
                               lambda bg, qb: ((bg // G) * nqb + qb, bg % G)),
        out_shape=jax.ShapeDtypeStruct((batch * seq, N_Q_HEADS * HEAD_DIM), BF16),
        scratch_shapes=[pltpu.VMEM((qblk, HEADS_PER_GROUP * qblk), F32),
                        pltpu.VMEM((qblk, HEADS_PER_GROUP * qblk), BF16)],
        compiler_params=_cparams(2),
        name="sel_win_attn",
    )(qT, sel_biasT, ks_aug, vsT_aug, kw, vwT_aug, o_cmpT, gatesT)


def _out_proj_kernel(a_ref, c_ref, x_ref, wa_ref, wc_ref, g_ref, b_ref, o_ref, *, alpha):
    mix = _dot(a_ref[...], wa_ref[...]) + _dot(c_ref[...], wc_ref[...])
    o_ref[...] = _layer_norm(alpha * x_ref[...] + mix, g_ref[...], b_ref[...])


def _out_proj(attn, conv, x2d, w_attn, w_conv, g, b, *, alpha, tm):
    T, d_model = x2d.shape
    row = lambda i: (i, 0)
    return pl.pallas_call(
        functools.partial(_out_proj_kernel, alpha=alpha),
        grid=(T // tm,),
        in_specs=[pl.BlockSpec((tm, attn.shape[1]), row), pl.BlockSpec((tm, conv.shape[1]), row),
                  pl.BlockSpec((tm, d_model), row),
                  _resident(w_attn.shape), _resident(w_conv.shape),
                  _resident(g.shape), _resident(b.shape)],
        out_specs=pl.BlockSpec((tm, d_model), row),
        out_shape=jax.ShapeDtypeStruct((T, d_model), F32),
        compiler_params=_cparams(1),
        name="out_proj_deepnorm",
    )(attn, conv, x2d, w_attn, w_conv, g, b)


def _ffn_kernel(h_ref, w1_ref, w2_ref, g_ref, b_ref, o_ref, hb_ref, *, alpha):
    j = pl.program_id(1)

    @pl.when(j == 0)
    def _():
        hb_ref[...] = h_ref[...].astype(BF16)
        o_ref[...] = jnp.zeros_like(o_ref)

    a = jnp.maximum(_dot(hb_ref[...], w1_ref[...]), 0.0)
    o_ref[...] += _dot((a * a).astype(BF16), w2_ref[...])

    @pl.when(j == pl.num_programs(1) - 1)
    def _():
        o_ref[...] = _layer_norm(alpha * h_ref[...] + o_ref[...], g_ref[...], b_ref[...])


def _ffn(h, w1, w2, g, b, *, alpha, tm, tf):
    T, d_model = h.shape
    d_ff = w1.shape[1]
    return pl.pallas_call(
        functools.partial(_ffn_kernel, alpha=alpha),
        grid=(T // tm, d_ff // tf),
        in_specs=[pl.BlockSpec((tm, d_model), lambda i, j: (i, 0)),
                  pl.BlockSpec((d_model, tf), lambda i, j: (0, j)),
                  pl.BlockSpec((tf, d_model), lambda i, j: (j, 0)),
                  _resident(g.shape), _resident(b.shape)],
        out_specs=pl.BlockSpec((tm, d_model), lambda i, j: (i, 0)),
        out_shape=jax.ShapeDtypeStruct((T, d_model), F32),
        scratch_shapes=[pltpu.VMEM((tm, d_model), BF16)],
        compiler_params=pltpu.CompilerParams(dimension_semantics=("parallel", "arbitrary"),
                                             vmem_limit_bytes=VMEM_LIMIT_BYTES),
        name="ffn_deepnorm",
    )(h, w1, w2, g, b)


def _rope_tables(pos, reps):
    half = HEAD_DIM // 2
    inv_freq = ROPE_THETA ** (-jnp.arange(half, dtype=F32) / half)
    ang = pos.astype(F32)[:, None] * inv_freq[None, :]
    cos, sin = jnp.cos(ang), jnp.sin(ang)
    return jnp.tile(jnp.concatenate([cos, cos], axis=1), (1, reps)), \
        jnp.tile(jnp.concatenate([-sin, sin], axis=1), (1, reps))


def _tiles(seq):
    return dict(in_proj=min(512, seq), out_proj=min(512, seq),
                ffn_rows=min(512, seq), ffn_cols=1024)


def _mixer(x2d, w_in, cmp_pe_k, cmp_w1_k, cmp_w2_k, cmp_pe_v, cmp_w1_v, cmp_w2_v,
           conv_w, conv_b, conv_ln_g, conv_ln_b, *, batch, seq):
    tiles = _tiles(seq)
    attn_w = N_Q_HEADS * HEAD_DIM
    kv_end = attn_w + 6 * KV_WIDTH
    gate_end = kv_end + N_Q_HEADS * N_BRANCH
    conv_width = (w_in.shape[1] - gate_end) // 2
    wq = w_in[:, :attn_w].astype(BF16)
    wkv = w_in[:, attn_w:kv_end].astype(BF16)
    wg = jnp.pad(w_in[:, kv_end:gate_end], ((0, 0), (0, 128 - (gate_end - kv_end)))).astype(BF16)
    wa = w_in[:, gate_end:gate_end + conv_width].astype(BF16)
    wgt = w_in[:, gate_end + conv_width:].astype(BF16)
    cos_t, sin_t = _rope_tables(jnp.arange(seq), 2)

    qT, k_cmp, v_cmp, ks_aug, vsT_aug, k_win, vwT_aug, gatesT, conv = _in_proj(
        x2d, wq, wkv, wg, wa, wgt, cos_t, sin_t,
        conv_w.reshape(CONV_KERNEL, conv_width), conv_b.reshape(1, -1),
        conv_ln_g.reshape(1, -1), conv_ln_b.reshape(1, -1),
        batch=batch, seq=seq, tm=tiles["in_proj"])

    BG = batch * N_KV_GROUPS
    n_chunks = seq // CMP_STRIDE
    chunk_w = CMP_STRIDE * HEAD_DIM
    cmp_end = jnp.arange(n_chunks) * CMP_STRIDE + CMP_BLOCK - 1
    cos_c, sin_c = _rope_tables(cmp_end, 2)
    pad_cols = lambda w2: jnp.pad(w2, ((0, 0), (0, 128 - HEAD_DIM)))
    swap = lambda w2: jnp.concatenate([w2[:, HEAD_DIM // 2:], w2[:, :HEAD_DIM // 2]], axis=1)

    def compress(t, pe, w1, w2, rotate):
        return _compress(t.reshape(BG, seq, HEAD_DIM), pe.reshape(2, chunk_w),
                         w1[:CMP_STRIDE].reshape(chunk_w, -1), w1[CMP_STRIDE:].reshape(chunk_w, -1),
                         pad_cols(w2), pad_cols(swap(w2)), cos_c, sin_c, rotate=rotate)

    kc = compress(k_cmp, cmp_pe_k, cmp_w1_k, cmp_w2_k, True)
    vcT = compress(v_cmp, cmp_pe_v, cmp_w1_v, cmp_w2_v, False)

    n_slc = seq // SEL_BLOCK
    cmp_start = jnp.arange(n_chunks) * CMP_STRIDE
    slc_start = jnp.arange(SEL_LANES) * SEL_BLOCK
    overlapT = ((cmp_start[None, :] < slc_start[:, None] + SEL_BLOCK)
                & (cmp_start[None, :] + CMP_BLOCK - 1 >= slc_start[:, None])
                & (jnp.arange(n_chunks)[None, :] < n_chunks - 1)
                & (jnp.arange(SEL_LANES)[:, None] < n_slc)).astype(F32)
    o_cmpT, sel_biasT = _cmp_attn(qT, kc, vcT, overlapT, batch=batch, seq=seq,
                                  n_top=min(N_SEL, n_slc))

    flat = lambda a: a.reshape((BG,) + a.shape[2:])
    attn = _sel_win_attn(qT, sel_biasT, flat(ks_aug), flat(vsT_aug), flat(k_win), flat(vwT_aug),
                         o_cmpT, flat(gatesT), batch=batch, seq=seq)
    return attn, conv


def kernel(x, w_in, cmp_pe_k, cmp_w1_k, cmp_w2_k, cmp_pe_v, cmp_w1_v, cmp_w2_v, conv_w, conv_b,
           conv_ln_g, conv_ln_b, w_out, ln1_g, ln1_b, w_ff1, w_ff2, ln2_g, ln2_b):
    batch, seq, d_model = x.shape
    depth = w_in.shape[0]
    alpha = (2.0 * depth) ** 0.25
    tiles = _tiles(seq)
    attn_w = N_Q_HEADS * HEAD_DIM
    h = x.reshape(batch * seq, d_model)
    for l in range(depth):
        attn, conv = _mixer(h, w_in[l], cmp_pe_k[l], cmp_w1_k[l], cmp_w2_k[l],
                            cmp_pe_v[l], cmp_w1_v[l], cmp_w2_v[l],
                            conv_w[l], conv_b[l], conv_ln_g[l], conv_ln_b[l],
                            batch=batch, seq=seq)
        h = _out_proj(attn, conv, h, w_out[l, :attn_w].astype(BF16), w_out[l, attn_w:].astype(BF16),
                      ln1_g[l].reshape(1, -1), ln1_b[l].reshape(1, -1),
                      alpha=alpha, tm=tiles["out_proj"])
        h = _ffn(h, w_ff1[l].astype(BF16), w_ff2[l].astype(BF16),
                 ln2_g[l].reshape(1, -1), ln2_b[l].reshape(1, -1),
                 alpha=alpha, tm=tiles["ffn_rows"], tf=tiles["ffn_cols"])
    return h.reshape(batch, seq, d_model)
```

```python
import functools
import math

import jax
import jax.numpy as jnp
from jax import lax
from jax.experimental import pallas as pl
from jax.experimental.pallas import tpu as pltpu

F32 = jnp.float32
BF16 = jnp.bfloat16

HEAD_DIM = 64
N_KV_GROUPS = 4
HEADS_PER_GROUP = 4
N_Q_HEADS = N_KV_GROUPS * HEADS_PER_GROUP
GROUP_WIDTH = HEADS_PER_GROUP * HEAD_DIM
N_BRANCH = 3
KV_WIDTH = N_KV_GROUPS * HEAD_DIM
CMP_BLOCK = 32
CMP_STRIDE = 16
SEL_BLOCK = 64
N_SEL = 16
WINDOW = 512
NEG_INF = -1.0e30
MASK_BIAS = -(2.0 ** 100)
CONV_KERNEL = 31
CONV_HALO = 32
CONV_ROWS = 128
ROPE_THETA = 10000.0
LN_EPS = 1e-5
SEL_LANES = 128
CMP_Q_BLOCK = 512
SEL_Q_BLOCK = 512
WIN_Q_BLOCK = 256
CMP_ROW_CHUNK = 128
GATES_PER_GROUP = HEADS_PER_GROUP * N_BRANCH
VMEM_LIMIT_BYTES = 56 * 1024 * 1024


def _cparams(n_axes):
    return pltpu.CompilerParams(dimension_semantics=("parallel",) * n_axes,
                                vmem_limit_bytes=VMEM_LIMIT_BYTES)


def _resident(shape):
    nd = len(shape)
    return pl.BlockSpec(shape, lambda *_: (0,) * nd, pipeline_mode=pl.Buffered(1))


def _dot(a, b):
    return jnp.dot(a, b, preferred_element_type=F32)


def _dot_f32(a, b):
    return jnp.dot(a, b, preferred_element_type=F32, precision=lax.Precision.HIGHEST)


def _dot_split(a, b):
    a_hi, b_hi = a.astype(BF16), b.astype(BF16)
    a_lo = (a - a_hi.astype(F32)).astype(BF16)
    b_lo = (b - b_hi.astype(F32)).astype(BF16)
    return _dot(a_hi, b_hi) + (_dot(a_hi, b_lo) + _dot(a_lo, b_hi))


def _layer_norm(y, g, b):
    mu = jnp.mean(y, axis=-1, keepdims=True)
    d = y - mu
    var = jnp.mean(d * d, axis=-1, keepdims=True)
    return d * lax.rsqrt(var + LN_EPS) * g + b


def _conv_ln_swish(win_ref, r0, rows, w_ref, b_ref, g_ref, beta_ref):
    span = CONV_HALO + rows
    cols = []
    for c in range(win_ref.shape[1] // 128):
        sl = slice(c * 128, (c + 1) * 128)
        win = win_ref[r0:r0 + span, sl]
        acc = jnp.zeros((rows, 128), F32)
        for r in range(8):
            shifted = win if r == 0 else pltpu.roll(win, span - r, 0)
            for j in range(CONV_KERNEL):
                off = CONV_HALO - (CONV_KERNEL - 1) + j
                if off % 8 == r:
                    acc = acc + shifted[off - r:off - r + rows] * w_ref[j:j + 1, sl]
        cols.append(acc)
    y = jnp.concatenate(cols, axis=1) + b_ref[...]
    y = _layer_norm(y, g_ref[...], beta_ref[...])
    return y * jax.nn.sigmoid(y)


def _in_proj_kernel(x_ref, wq_ref, wkv_ref, wg_ref, wa_ref, wgt_ref, cos_ref, sin_ref,
                    cw_ref, cb_ref, cg_ref, cbeta_ref,
                    qT_ref, kc_ref, vc_ref, ks_ref, vsT_ref, kw_ref, vwT_ref, gT_ref, conv_ref,
                    win_ref, *, tm, n_row_tiles):
    @pl.when(pl.program_id(0) % n_row_tiles == 0)
    def _():
        win_ref[0:CONV_HALO, :] = jnp.zeros((CONV_HALO, win_ref.shape[1]), F32)

    xb = x_ref[...].astype(BF16)
    cos = cos_ref[...]
    sin = sin_ref[...]
    lane = lax.broadcasted_iota(jnp.int32, (tm, 128), 1)
    first_half = (lane & (HEAD_DIM - 1)) < (HEAD_DIM // 2)

    def rope(t):
        partner = jnp.where(first_half, pltpu.roll(t, 128 - HEAD_DIM // 2, 1),
                            pltpu.roll(t, HEAD_DIM // 2, 1))
        return t * cos + partner * sin

    for c in range(wa_ref.shape[1] // 512):
        sl = slice(c * 512, (c + 1) * 512)
        a = _dot(xb, wa_ref[:, sl])
        gt = _dot(xb, wgt_ref[:, sl])
        win_ref[CONV_HALO:, sl] = a * jax.nn.sigmoid(gt)
    for r0 in range(0, tm, CONV_ROWS):
        conv_ref[r0:r0 + CONV_ROWS, :] = _conv_ln_swish(
            win_ref, r0, CONV_ROWS, cw_ref, cb_ref, cg_ref, cbeta_ref).astype(conv_ref.dtype)
    win_ref[0:CONV_HALO, :] = win_ref[tm:tm + CONV_HALO, :]

    q_scale = HEAD_DIM ** -0.5 * math.log2(math.e)
    for c in range(wq_ref.shape[1] // 512):
        t = _dot(xb, wq_ref[:, c * 512:(c + 1) * 512])
        t = jnp.concatenate([rope(t[:, cc * 128:(cc + 1) * 128]) for cc in range(4)], axis=1)
        qT_ref[c * 512:(c + 1) * 512, :] = (t * q_scale).T.astype(BF16)

    kv = [_dot(xb, wkv_ref[:, j * KV_WIDTH:(j + 1) * KV_WIDTH]) for j in range(6)]
    k_cmp, v_cmp, k_sel, v_sel, k_win, v_win = kv
    k_sel = jnp.concatenate([rope(k_sel[:, :128]), rope(k_sel[:, 128:])], axis=1)
    k_win = jnp.concatenate([rope(k_win[:, :128]), rope(k_win[:, 128:])], axis=1)
    v_selT = v_sel.T.astype(BF16)
    v_winT = v_win.T.astype(BF16)

    s_start = (pl.program_id(0) % n_row_tiles) * tm
    key_blk = (s_start + lax.broadcasted_iota(jnp.int32, (tm, SEL_LANES), 0)) // SEL_BLOCK
    onehot = (key_blk == lax.broadcasted_iota(jnp.int32, (tm, SEL_LANES), 1)).astype(BF16)
    ones = jnp.ones((HEAD_DIM, tm), BF16)
    for g in range(N_KV_GROUPS):
        sl = slice(g * HEAD_DIM, (g + 1) * HEAD_DIM)
        kc_ref[g] = k_cmp[:, sl]
        vc_ref[g] = v_cmp[:, sl]
        ks_ref[g, :, 0:HEAD_DIM] = k_sel[:, sl].astype(BF16)
        ks_ref[g, :, HEAD_DIM:] = onehot
        kw_ref[g] = k_win[:, sl].astype(BF16)
        vsT_ref[g, 0:HEAD_DIM, :] = v_selT[sl]
        vsT_ref[g, HEAD_DIM:, :] = ones
        vwT_ref[g, 0:HEAD_DIM, :] = v_winT[sl]
        vwT_ref[g, HEAD_DIM:, :] = ones

    gatesT = jax.nn.sigmoid(_dot(xb, wg_ref[...])).T
    for g in range(N_KV_GROUPS):
        gT_ref[g] = gatesT[g * GATES_PER_GROUP:(g + 1) * GATES_PER_GROUP]


def _in_proj(x2d, wq, wkv, wg, wa, wgt, cos_t, sin_t, conv_w, conv_b, conv_g, conv_beta, *,
             batch, seq, tm):
    T, d_model = x2d.shape
    n_row_tiles = seq // tm
    G = N_KV_GROUPS
    row = lambda i: (i, 0)
    tab = lambda i: (i % n_row_tiles, 0)
    per_group = lambda i: (i // n_row_tiles, 0, i % n_row_tiles, 0)
    per_group_t = lambda i: (i // n_row_tiles, 0, 0, i % n_row_tiles)

    def grp(width, dtype):
        return (jax.ShapeDtypeStruct((batch, G, seq, width), dtype),
                pl.BlockSpec((None, G, tm, width), per_group))

    def grp_t(height, dtype):
        return (jax.ShapeDtypeStruct((batch, G, height, seq), dtype),
                pl.BlockSpec((None, G, height, tm), per_group_t))

    outs = [
        (jax.ShapeDtypeStruct((wq.shape[1], T), BF16),
         pl.BlockSpec((wq.shape[1], tm), lambda i: (0, i))),
        grp(HEAD_DIM, F32), grp(HEAD_DIM, F32),
        grp(HEAD_DIM + SEL_LANES, BF16), grp_t(2 * HEAD_DIM, BF16),
        grp(HEAD_DIM, BF16), grp_t(2 * HEAD_DIM, BF16),
        grp_t(GATES_PER_GROUP, F32),
        (jax.ShapeDtypeStruct((T, wa.shape[1]), BF16), pl.BlockSpec((tm, wa.shape[1]), row)),
    ]
    return pl.pallas_call(
        functools.partial(_in_proj_kernel, tm=tm, n_row_tiles=n_row_tiles),
        grid=(T // tm,),
        in_specs=[pl.BlockSpec((tm, d_model), row),
                  _resident(wq.shape), _resident(wkv.shape), _resident(wg.shape),
                  _resident(wa.shape), _resident(wgt.shape),
                  pl.BlockSpec((tm, 128), tab), pl.BlockSpec((tm, 128), tab),
                  _resident(conv_w.shape), _resident(conv_b.shape),
                  _resident(conv_g.shape), _resident(conv_beta.shape)],
        out_specs=[o[1] for o in outs],
        out_shape=[o[0] for o in outs],
        scratch_shapes=[pltpu.VMEM((CONV_HALO + tm, wa.shape[1]), F32)],
        compiler_params=pltpu.CompilerParams(dimension_semantics=("arbitrary",),
                                             vmem_limit_bytes=VMEM_LIMIT_BYTES),
        name="in_proj",
    )(x2d, wq, wkv, wg, wa, wgt, cos_t, sin_t, conv_w, conv_b, conv_g, conv_beta)


def _compress_kernel(c_ref, pe_ref, w1a_ref, w1b_ref, w2_ref, w2s_ref, cos_ref, sin_ref,
                     o_ref, *, rotate):
    n_chunks = c_ref.shape[0] // CMP_STRIDE
    c = jnp.concatenate([c_ref[pl.ds(l, n_chunks, stride=CMP_STRIDE), :]
                         for l in range(CMP_STRIDE)], axis=1)
    a = _dot_split(c + pe_ref[0:1, :], w1a_ref[...])
    b = _dot_split(c + pe_ref[1:2, :], w1b_ref[...])
    hid = jax.nn.gelu(a + pltpu.roll(b, n_chunks - 1, 0))
    out = _dot_split(hid, w2_ref[...])
    if rotate:
        out = out * cos_ref[...] + _dot_split(hid, w2s_ref[...]) * sin_ref[...]
        o_ref[...] = out[:, :HEAD_DIM].astype(o_ref.dtype)
    else:
        o_ref[...] = out.T[:HEAD_DIM].astype(o_ref.dtype)


def _compress(tokens, pe2, w1a, w1b, w2, w2s, cos_c, sin_c, *, rotate):
    BG, seq, _ = tokens.shape
    n_chunks = seq // CMP_STRIDE
    out_block = (None, n_chunks, HEAD_DIM) if rotate else (None, HEAD_DIM, n_chunks)
    return pl.pallas_call(
        functools.partial(_compress_kernel, rotate=rotate),
        grid=(BG,),
        in_specs=[pl.BlockSpec((None, seq, HEAD_DIM), lambda i: (i, 0, 0)),
                  _resident(pe2.shape), _resident(w1a.shape), _resident(w1b.shape),
                  _resident(w2.shape), _resident(w2s.shape),
                  _resident(cos_c.shape), _resident(sin_c.shape)],
        out_specs=pl.BlockSpec(out_block, lambda i: (i, 0, 0)),
        out_shape=jax.ShapeDtypeStruct((BG,) + out_block[1:], BF16),
        compiler_params=_cparams(1),
        name="compress_k" if rotate else "compress_v",
    )(tokens, pe2, w1a, w1b, w2, w2s, cos_c, sin_c)


def _heads_to_lanes(qT):
    return jnp.concatenate(
        [qT[h * HEAD_DIM:(h + 1) * HEAD_DIM] for h in range(HEADS_PER_GROUP)], axis=1)


def _tile_heads(a):
    return jnp.concatenate([a] * HEADS_PER_GROUP, axis=1)


def _cmp_attn_kernel(qT_ref, kc_ref, vcT_ref, ovT_ref, ocT_ref, sbT_ref, *, n_top):
    qblk = CMP_Q_BLOCK
    qb = pl.program_id(1)
    s0 = qb * qblk
    cols = HEADS_PER_GROUP * qblk
    n_cmp = kc_ref.shape[0]
    q_stackT = _heads_to_lanes(qT_ref[...])
    t_lane = s0 + (lax.broadcasted_iota(jnp.int32, (1, cols), 1) & (qblk - 1))

    def attend(n):
        base = (lax.broadcasted_iota(jnp.int32, (n, qblk), 0) * CMP_STRIDE
                - lax.broadcasted_iota(jnp.int32, (n, qblk), 1))
        bias = jnp.where(base <= s0 - (CMP_BLOCK - 1), 0.0, NEG_INF)
        s = _dot(kc_ref[0:n, :], q_stackT) + _tile_heads(bias)
        m = jnp.max(s, axis=0, keepdims=True)
        e = jnp.exp2(s - m)
        den = jnp.sum(e, axis=0, keepdims=True)
        p = e * jnp.where(t_lane >= CMP_BLOCK - 1, 1.0 / den, 0.0)
        o = _dot(vcT_ref[:, 0:n], p.astype(BF16))
        for h in range(HEADS_PER_GROUP):
            ocT_ref[h] = o[:, h * qblk:(h + 1) * qblk]
        p_sum = p[:, 0:qblk]
        for h in range(1, HEADS_PER_GROUP):
            p_sum = p_sum + p[:, h * qblk:(h + 1) * qblk]
        n_sel = n * CMP_STRIDE // SEL_BLOCK
        imp = _dot_f32(ovT_ref[0:n_sel, 0:n], p_sum)

        tq = s0 + lax.broadcasted_iota(jnp.int32, (n_sel, qblk), 1)
        blk = lax.broadcasted_iota(jnp.int32, (n_sel, qblk), 0)
        cur = tq // SEL_BLOCK
        forced = (blk == 0) | (blk == cur) | (blk == cur - 1)
        v = jnp.where(forced, -jnp.inf, jnp.where(blk * SEL_BLOCK <= tq, imp, -1.0))
        idx = blk.astype(F32)
        sel = forced
        for _ in range(n_top - 3):
            vmax = jnp.max(v, axis=0, keepdims=True)
            first = jnp.min(jnp.where(v == vmax, idx, float(SEL_LANES)), axis=0, keepdims=True)
            pick = idx == first
            sel = sel | pick
            v = jnp.where(pick, -jnp.inf, v)
        sbT_ref[0:n_sel, :] = jnp.where(sel, 0.0, MASK_BIAS).astype(BF16)
        if n_sel < SEL_LANES:
            sbT_ref[n_sel:, :] = jnp.full((SEL_LANES - n_sel, qblk), MASK_BIAS, BF16)

    chunks_needed = pl.cdiv((qb + 1) * (qblk // CMP_STRIDE), CMP_ROW_CHUNK)
    for k in range(1, n_cmp // CMP_ROW_CHUNK + 1):
        pl.when(chunks_needed == k)(functools.partial(attend, k * CMP_ROW_CHUNK))


def _cmp_attn(qT, kc, vcT, overlapT, *, batch, seq, n_top):
    BG, n_cmp, _ = kc.shape
    G = N_KV_GROUPS
    qblk = CMP_Q_BLOCK
    nqb = seq // qblk
    return pl.pallas_call(
        functools.partial(_cmp_attn_kernel, n_top=n_top),
        grid=(BG, nqb),
        in_specs=[pl.BlockSpec((GROUP_WIDTH, qblk),
                               lambda bg, qb: (bg % G, (bg // G) * nqb + qb)),
                  pl.BlockSpec((None, n_cmp, HEAD_DIM), lambda bg, qb: (bg, 0, 0)),
                  pl.BlockSpec((None, HEAD_DIM, n_cmp), lambda bg, qb: (bg, 0, 0)),
                  _resident(overlapT.shape)],
        out_specs=[pl.BlockSpec((None, HEADS_PER_GROUP, HEAD_DIM, qblk),
                                lambda bg, qb: (bg, 0, 0, qb)),
                   pl.BlockSpec((None, SEL_LANES, qblk), lambda bg, qb: (bg, 0, qb))],
        out_shape=[jax.ShapeDtypeStruct((BG, HEADS_PER_GROUP, HEAD_DIM, seq), F32),
                   jax.ShapeDtypeStruct((BG, SEL_LANES, seq), BF16)],
        compiler_params=_cparams(2),
        name="cmp_attn_select",
    )(qT, kc, vcT, overlapT)


def _sel_win_kernel(qT_ref, sbT_ref, ks_ref, vsT_ref, kw_ref, vwT_ref, ocT_ref, gT_ref, o_ref,
                    s_ref, p_ref):
    qblk = SEL_Q_BLOCK
    s0 = pl.program_id(1) * qblk
    cols = HEADS_PER_GROUP * qblk
    qT = qT_ref[...]
    biasT = sbT_ref[...]
    q_augT = jnp.concatenate(
        [jnp.concatenate([qT[h * HEAD_DIM:(h + 1) * HEAD_DIM], biasT], axis=0)
         for h in range(HEADS_PER_GROUP)], axis=1)

    def tile_start(kt):
        return pl.multiple_of(kt * qblk, qblk)

    def scores(kt):
        return _dot(ks_ref[pl.ds(tile_start(kt), qblk), :], q_augT)

    def pv(kt, p):
        return _dot(vsT_ref[:, pl.ds(tile_start(kt), qblk)], p)

    def softmax_step(s, kt, m, masked):
        if masked:
            d = (lax.broadcasted_iota(jnp.int32, (qblk, qblk), 0)
                 - lax.broadcasted_iota(jnp.int32, (qblk, qblk), 1))
            s = s + _tile_heads(jnp.where(d <= s0 - kt * qblk, 0.0, MASK_BIAS))
        m_new = jnp.maximum(m, jnp.max(s, axis=0, keepdims=True))
        return m_new, jnp.exp2(m - m_new), jnp.exp2(s - m_new).astype(BF16)

    def pair(i, carry):
        m, acc = carry
        pv_prev = pv(jnp.maximum(2 * i - 1, 0), p_ref[...])
        s_odd = scores(2 * i + 1)
        m, alpha, p = softmax_step(s_ref[...], 2 * i, m, False)
        acc = alpha * (acc + pv_prev) + pv(2 * i, p)
        s_ref[...] = scores(2 * i + 2)
        m, alpha, p = softmax_step(s_odd, 2 * i + 1, m, False)
        p_ref[...] = p
        return m, alpha * acc

    def tail(kt, n_full, carry):
        m, acc = carry
        pv_prev = pv(jnp.maximum(kt - 1, 0), p_ref[...])
        m, alpha, p = softmax_step(s_ref[...], kt, m, True)
        acc = alpha * (acc + pv_prev) + pv(kt, p)

        def diagonal(c):
            m, acc = c
            m, alpha, p = softmax_step(scores(kt + 1), kt + 1, m, True)
            return m, alpha * acc + pv(kt + 1, p)

        return lax.cond(kt < n_full, diagonal, lambda c: c, (m, acc))

    s_ref[...] = scores(0)
    p_ref[...] = jnp.zeros_like(p_ref)
    wq = WIN_Q_BLOCK
    d = (lax.broadcasted_iota(jnp.int32, (wq, wq), 0)
         - lax.broadcasted_iota(jnp.int32, (wq, wq), 1))

    def window(sub):
        sq = s0 + sub * wq
        q_sub = _heads_to_lanes(qT[:, sub * wq:(sub + 1) * wq])
        wstart = pl.multiple_of(jnp.maximum(sq - WINDOW, 0), wq)
        sw = []
        for c in range(WINDOW // wq + 1):
            off = sq - wstart - c * wq
            wbias = jnp.where((d <= off) & (d > off - WINDOW), 0.0, NEG_INF)
            sw.append(_dot(kw_ref[pl.ds(wstart + c * wq, wq), :], q_sub) + _tile_heads(wbias))
        sw = jnp.concatenate(sw, axis=0)
        pw = jnp.exp2(sw - jnp.max(sw, axis=0, keepdims=True)).astype(BF16)
        accw = _dot(vwT_ref[:, pl.ds(wstart, WINDOW + wq)], pw)
        return accw[:HEAD_DIM] * (1.0 / accw[HEAD_DIM:HEAD_DIM + 1])

    o_w_subs = [window(sub) for sub in range(qblk // wq)]

    n_full = pl.program_id(1)
    n_main = n_full // 2
    init = (jnp.full((1, cols), NEG_INF, F32), jnp.zeros((2 * HEAD_DIM, cols), F32))
    carry = lax.fori_loop(0, n_main, pair, init)
    _, acc = tail(2 * n_main, n_full, carry)
    o_s = acc[:HEAD_DIM] * (1.0 / acc[HEAD_DIM:HEAD_DIM + 1])

    gT = gT_ref[...]
    outs = []
    for h in range(HEADS_PER_GROUP):
        c = slice(h * qblk, (h + 1) * qblk)
        gate = lambda br: gT[h * N_BRANCH + br:h * N_BRANCH + br + 1, :]
        o_w = jnp.concatenate([o[:, h * wq:(h + 1) * wq] for o in o_w_subs], axis=1)
        outs.append(gate(0) * ocT_ref[h] + gate(1) * o_s[:, c] + gate(2) * o_w)
    o_ref[...] = jnp.concatenate(outs, axis=0).T.astype(o_ref.dtype)


def _sel_win_attn(qT, sel_biasT, ks_aug, vsT_aug, kw, vwT_aug, o_cmpT, gatesT, *,
                  batch, seq):
    G = N_KV_GROUPS
    BG = batch * G
    qblk = SEL_Q_BLOCK
    nqb = seq // qblk
    whole = lambda a: pl.BlockSpec((None,) + a.shape[1:], lambda bg, qb: (bg, 0, 0),
                                   pipeline_mode=pl.Buffered(1))
    return pl.pallas_call(
        _sel_win_kernel,
        grid=(BG, nqb),
        in_specs=[pl.BlockSpec((GROUP_WIDTH, qblk),
                               lambda bg, qb: (bg % G, (bg // G) * nqb + qb)),
                  pl.BlockSpec((None, SEL_LANES, qblk), lambda bg, qb: (bg, 0, qb)),
                  whole(ks_aug), whole(vsT_aug), whole(kw), whole(vwT_aug),
                  pl.BlockSpec((None, HEADS_PER_GROUP, HEAD_DIM, qblk),
                               lambda bg, qb: (bg, 0, 0, qb)),
                  pl.BlockSpec((None, GATES_PER_GROUP, qblk), lambda bg, qb: (bg, 0, qb))],
        out_specs=pl.BlockSpec((qblk, GROUP_WIDTH),
                               lambda bg, qb: ((bg // G) * nqb + qb, bg % G)),
        out_shape=jax.ShapeDtypeStruct((batch * seq, N_Q_HEADS * HEAD_DIM), BF16),
        scratch_shapes=[pltpu.VMEM((qblk, HEADS_PER_GROUP * qblk), F32),
                        pltpu.VMEM((qblk, HEADS_PER_GROUP * qblk), BF16)],
        compiler_params=_cparams(2),
        name="sel_win_attn",
    )(qT, sel_biasT, ks_aug, vsT_aug, kw, vwT_aug, o_cmpT, gatesT)


def _out_proj_kernel(a_ref, c_ref, x_ref, wa_ref, wc_ref, g_ref, b_ref, o_ref, *, alpha):
    mix = _dot(a_ref[...], wa_ref[...]) + _dot(c_ref[...], wc_ref[...])
    o_ref[...] = _layer_norm(alpha * x_ref[...] + mix, g_ref[...], b_ref[...])


def _out_proj(attn, conv, x2d, w_attn, w_conv, g, b, *, alpha, tm):
    T, d_model = x2d.shape
    row = lambda i: (i, 0)
    return pl.pallas_call(
        functools.partial(_out_proj_kernel, alpha=alpha),
        grid=(T // tm,),
        in_specs=[pl.BlockSpec((tm, attn.shape[1]), row), pl.BlockSpec((tm, conv.shape[1]), row),
                  pl.BlockSpec((tm, d_model), row),
                  _resident(w_attn.shape), _resident(w_conv.shape),
                  _resident(g.shape), _resident(b.shape)],
        out_specs=pl.BlockSpec((tm, d_model), row),
        out_shape=jax.ShapeDtypeStruct((T, d_model), F32),
        compiler_params=_cparams(1),
        name="out_proj_deepnorm",
    )(attn, conv, x2d, w_attn, w_conv, g, b)


def _ffn_kernel(h_ref, w1_ref, w2_ref, g_ref, b_ref, o_ref, hb_ref, *, alpha):
    j = pl.program_id(1)

    @pl.when(j == 0)
    def _():
        hb_ref[...] = h_ref[...].astype(BF16)
        o_ref[...] = jnp.zeros_like(o_ref)

    a = jnp.maximum(_dot(hb_ref[...], w1_ref[...]), 0.0)
    o_ref[...] += _dot((a * a).astype(BF16), w2_ref[...])

    @pl.when(j == pl.num_programs(1) - 1)
    def _():
        o_ref[...] = _layer_norm(alpha * h_ref[...] + o_ref[...], g_ref[...], b_ref[...])


def _ffn(h, w1, w2, g, b, *, alpha, tm, tf):
    T, d_model = h.shape
    d_ff = w1.shape[1]
    return pl.pallas_call(
        functools.partial(_ffn_kernel, alpha=alpha),
        grid=(T // tm, d_ff // tf),
        in_specs=[pl.BlockSpec((tm, d_model), lambda i, j: (i, 0)),
                  pl.BlockSpec((d_model, tf), lambda i, j: (0, j)),
                  pl.BlockSpec((tf, d_model), lambda i, j: (j, 0)),
                  _resident(g.shape), _resident(b.shape)],
        out_specs=pl.BlockSpec((tm, d_model), lambda i, j: (i, 0)),
        out_shape=jax.ShapeDtypeStruct((T, d_model), F32),
        scratch_shapes=[pltpu.VMEM((tm, d_model), BF16)],
        compiler_params=pltpu.CompilerParams(dimension_semantics=("parallel", "arbitrary"),
                                             vmem_limit_bytes=VMEM_LIMIT_BYTES),
        name="ffn_deepnorm",
    )(h, w1, w2, g, b)


def _rope_tables(pos, reps):
    half = HEAD_DIM // 2
    inv_freq = ROPE_THETA ** (-jnp.arange(half, dtype=F32) / half)
    ang = pos.astype(F32)[:, None] * inv_freq[None, :]
    cos, sin = jnp.cos(ang), jnp.sin(ang)
    return jnp.tile(jnp.concatenate([cos, cos], axis=1), (1, reps)), \
        jnp.tile(jnp.concatenate([-sin, sin], axis=1), (1, reps))


def _tiles(seq):
    return dict(in_proj=min(512, seq), out_proj=min(512, seq),
                ffn_rows=min(512, seq), ffn_cols=1024)


def _mixer(x2d, w_in, cmp_pe_k, cmp_w1_k, cmp_w2_k, cmp_pe_v, cmp_w1_v, cmp_w2_v,
           conv_w, conv_b, conv_ln_g, conv_ln_b, *, batch, seq):
    tiles = _tiles(seq)
    attn_w = N_Q_HEADS * HEAD_DIM
    kv_end = attn_w + 6 * KV_WIDTH
    gate_end = kv_end + N_Q_HEADS * N_BRANCH
    conv_width = (w_in.shape[1] - gate_end) // 2
    wq = w_in[:, :attn_w].astype(BF16)
    wkv = w_in[:, attn_w:kv_end].astype(BF16)
    wg = jnp.pad(w_in[:, kv_end:gate_end], ((0, 0), (0, 128 - (gate_end - kv_end)))).astype(BF16)
    wa = w_in[:, gate_end:gate_end + conv_width].astype(BF16)
    wgt = w_in[:, gate_end + conv_width:].astype(BF16)
    cos_t, sin_t = _rope_tables(jnp.arange(seq), 2)

    qT, k_cmp, v_cmp, ks_aug, vsT_aug, k_win, vwT_aug, gatesT, conv = _in_proj(
        x2d, wq, wkv, wg, wa, wgt, cos_t, sin_t,
        conv_w.reshape(CONV_KERNEL, conv_width), conv_b.reshape(1, -1),
        conv_ln_g.reshape(1, -1), conv_ln_b.reshape(1, -1),
        batch=batch, seq=seq, tm=tiles["in_proj"])

    BG = batch * N_KV_GROUPS
    n_chunks = seq // CMP_STRIDE
    chunk_w = CMP_STRIDE * HEAD_DIM
    cmp_end = jnp.arange(n_chunks) * CMP_STRIDE + CMP_BLOCK - 1
    cos_c, sin_c = _rope_tables(cmp_end, 2)
    pad_cols = lambda w2: jnp.pad(w2, ((0, 0), (0, 128 - HEAD_DIM)))
    swap = lambda w2: jnp.concatenate([w2[:, HEAD_DIM // 2:], w2[:, :HEAD_DIM // 2]], axis=1)

    def compress(t, pe, w1, w2, rotate):
        return _compress(t.reshape(BG, seq, HEAD_DIM), pe.reshape(2, chunk_w),
                         w1[:CMP_STRIDE].reshape(chunk_w, -1), w1[CMP_STRIDE:].reshape(chunk_w, -1),
                         pad_cols(w2), pad_cols(swap(w2)), cos_c, sin_c, rotate=rotate)

    kc = compress(k_cmp, cmp_pe_k, cmp_w1_k, cmp_w2_k, True)
    vcT = compress(v_cmp, cmp_pe_v, cmp_w1_v, cmp_w2_v, False)

    n_slc = seq // SEL_BLOCK
    cmp_start = jnp.arange(n_chunks) * CMP_STRIDE
    slc_start = jnp.arange(SEL_LANES) * SEL_BLOCK
    overlapT = ((cmp_start[None, :] < slc_start[:, None] + SEL_BLOCK)
                & (cmp_start[None, :] + CMP_BLOCK - 1 >= slc_start[:, None])
                & (jnp.arange(n_chunks)[None, :] < n_chunks - 1)
                & (jnp.arange(SEL_LANES)[:, None] < n_slc)).astype(F32)
    o_cmpT, sel_biasT = _cmp_attn(qT, kc, vcT, overlapT, batch=batch, seq=seq,
                                  n_top=min(N_SEL, n_slc))

    flat = lambda a: a.reshape((BG,) + a.shape[2:])
    attn = _sel_win_attn(qT, sel_biasT, flat(ks_aug), flat(vsT_aug), flat(k_win), flat(vwT_aug),
                         o_cmpT, flat(gatesT), batch=batch, seq=seq)
    return attn, conv


def kernel(x, w_in, cmp_pe_k, cmp_w1_k, cmp_w2_k, cmp_pe_v, cmp_w1_v, cmp_w2_v, conv_w, conv_b,
           conv_ln_g, conv_ln_b, w_out, ln1_g, ln1_b, w_ff1, w_ff2, ln2_g, ln2_b):
    batch, seq, d_model = x.shape
    depth = w_in.shape[0]
    alpha = (2.0 * depth) ** 0.25
    tiles = _tiles(seq)
    attn_w = N_Q_HEADS * HEAD_DIM
    h = x.reshape(batch * seq, d_model)
    for l in range(depth):
        attn, conv = _mixer(h, w_in[l], cmp_pe_k[l], cmp_w1_k[l], cmp_w2_k[l],
                            cmp_pe_v[l], cmp_w1_v[l], cmp_w2_v[l],
                            conv_w[l], conv_b[l], conv_ln_g[l], conv_ln_b[l],
                            batch=batch, seq=seq)
        h = _out_proj(attn, conv, h, w_out[l, :attn_w].astype(BF16), w_out[l, attn_w:].astype(BF16),
                      ln1_g[l].reshape(1, -1), ln1_b[l].reshape(1, -1),
                      alpha=alpha, tm=tiles["out_proj"])
        h = _ffn(h, w_ff1[l].astype(BF16), w_ff2[l].astype(BF16),
                 ln2_g[l].reshape(1, -1), ln2_b[l].reshape(1, -1),
                 alpha=alpha, tm=tiles["ffn_rows"], tf=tiles["ffn_cols"])
    return h.reshape(batch, seq, d_model)
```

```python
import functools
import math

import jax
import jax.numpy as jnp
from jax import lax
from jax.experimental import pallas as pl
from jax.experimental.pallas import tpu as pltpu

F32 = jnp.float32
BF16 = jnp.bfloat16

HEAD_DIM = 64
N_KV_GROUPS = 4
HEADS_PER_GROUP = 4
N_Q_HEADS = N_KV_GROUPS * HEADS_PER_GROUP
GROUP_WIDTH = HEADS_PER_GROUP * HEAD_DIM
N_BRANCH = 3
KV_WIDTH = N_KV_GROUPS * HEAD_DIM
CMP_BLOCK = 32
CMP_STRIDE = 16
SEL_BLOCK = 64
N_SEL = 16
WINDOW = 512
NEG_INF = -1.0e30
MASK_BIAS = -(2.0 ** 100)
CONV_KERNEL = 31
CONV_HALO = 32
CONV_ROWS = 128
ROPE_THETA = 10000.0
LN_EPS = 1e-5
LANES = 128
SUBLANES = 8
PROJ_COLS = 512
SEL_LANES = LANES
CMP_Q_BLOCK = 512
SEL_Q_BLOCK = 512
WIN_Q_BLOCK = 256
CMP_ROW_CHUNK = 128
GATES_PER_GROUP = HEADS_PER_GROUP * N_BRANCH
VMEM_LIMIT_BYTES = 56 * 1024 * 1024


def _cparams(n_axes):
    return pltpu.CompilerParams(dimension_semantics=("parallel",) * n_axes,
                                vmem_limit_bytes=VMEM_LIMIT_BYTES)


def _resident(shape):
    nd = len(shape)
    return pl.BlockSpec(shape, lambda *_: (0,) * nd, pipeline_mode=pl.Buffered(1))


def _dot(a, b):
    return jnp.dot(a, b, preferred_element_type=F32)


def _dot_f32(a, b):
    return jnp.dot(a, b, preferred_element_type=F32, precision=lax.Precision.HIGHEST)


def _dot_split(a, b):
    a_hi, b_hi = a.astype(BF16), b.astype(BF16)
    a_lo = (a - a_hi.astype(F32)).astype(BF16)
    b_lo = (b - b_hi.astype(F32)).astype(BF16)
    return _dot(a_hi, b_hi) + (_dot(a_hi, b_lo) + _dot(a_lo, b_hi))


def _layer_norm(y, g, b):
    mu = jnp.mean(y, axis=-1, keepdims=True)
    d = y - mu
    var = jnp.mean(d * d, axis=-1, keepdims=True)
    return d * lax.rsqrt(var + LN_EPS) * g + b


def _conv_ln_swish(win_ref, r0, rows, w_ref, b_ref, g_ref, beta_ref):
    span = CONV_HALO + rows
    cols = []
    for c in range(win_ref.shape[1] // LANES):
        sl = slice(c * LANES, (c + 1) * LANES)
        win = win_ref[r0:r0 + span, sl]
        acc = jnp.zeros((rows, LANES), F32)
        for r in range(SUBLANES):
            shifted = win if r == 0 else pltpu.roll(win, span - r, 0)
            for j in range(CONV_KERNEL):
                off = CONV_HALO - (CONV_KERNEL - 1) + j
                if off % SUBLANES == r:
                    acc = acc + shifted[off - r:off - r + rows] * w_ref[j:j + 1, sl]
        cols.append(acc)
    y = jnp.concatenate(cols, axis=1) + b_ref[...]
    y = _layer_norm(y, g_ref[...], beta_ref[...])
    return y * jax.nn.sigmoid(y)


def _in_proj_kernel(x_ref, wq_ref, wkv_ref, wg_ref, wa_ref, wgt_ref, cos_ref, sin_ref,
                    cw_ref, cb_ref, cg_ref, cbeta_ref,
                    qT_ref, kc_ref, vc_ref, ks_ref, vsT_ref, kw_ref, vwT_ref, gT_ref, conv_ref,
                    win_ref, *, tm, n_row_tiles):
    @pl.when(pl.program_id(0) % n_row_tiles == 0)
    def _():
        win_ref[0:CONV_HALO, :] = jnp.zeros((CONV_HALO, win_ref.shape[1]), F32)

    xb = x_ref[...].astype(BF16)
    cos = cos_ref[...]
    sin = sin_ref[...]
    lane = lax.broadcasted_iota(jnp.int32, (tm, LANES), 1)
    first_half = (lane & (HEAD_DIM - 1)) < (HEAD_DIM // 2)

    def rope(t):
        partner = jnp.where(first_half, pltpu.roll(t, LANES - HEAD_DIM // 2, 1),
                            pltpu.roll(t, HEAD_DIM // 2, 1))
        return t * cos + partner * sin

    for c in range(wa_ref.shape[1] // PROJ_COLS):
        sl = slice(c * PROJ_COLS, (c + 1) * PROJ_COLS)
        a = _dot(xb, wa_ref[:, sl])
        gt = _dot(xb, wgt_ref[:, sl])
        win_ref[CONV_HALO:, sl] = a * jax.nn.sigmoid(gt)
    for r0 in range(0, tm, CONV_ROWS):
        conv_ref[r0:r0 + CONV_ROWS, :] = _conv_ln_swish(
            win_ref, r0, CONV_ROWS, cw_ref, cb_ref, cg_ref, cbeta_ref).astype(conv_ref.dtype)
    win_ref[0:CONV_HALO, :] = win_ref[tm:tm + CONV_HALO, :]

    q_scale = HEAD_DIM ** -0.5 * math.log2(math.e)
    for c in range(wq_ref.shape[1] // PROJ_COLS):
        sl = slice(c * PROJ_COLS, (c + 1) * PROJ_COLS)
        t = _dot(xb, wq_ref[:, sl])
        t = jnp.concatenate([rope(t[:, cc * LANES:(cc + 1) * LANES])
                             for cc in range(PROJ_COLS // LANES)], axis=1)
        qT_ref[sl, :] = (t * q_scale).T.astype(BF16)

    kv = [_dot(xb, wkv_ref[:, j * KV_WIDTH:(j + 1) * KV_WIDTH]) for j in range(6)]
    k_cmp, v_cmp, k_sel, v_sel, k_win, v_win = kv
    k_sel = jnp.concatenate([rope(k_sel[:, :LANES]), rope(k_sel[:, LANES:])], axis=1)
    k_win = jnp.concatenate([rope(k_win[:, :LANES]), rope(k_win[:, LANES:])], axis=1)
    v_selT = v_sel.T.astype(BF16)
    v_winT = v_win.T.astype(BF16)

    s_start = (pl.program_id(0) % n_row_tiles) * tm
    key_blk = (s_start + lax.broadcasted_iota(jnp.int32, (tm, SEL_LANES), 0)) // SEL_BLOCK
    onehot = (key_blk == lax.broadcasted_iota(jnp.int32, (tm, SEL_LANES), 1)).astype(BF16)
    ones = jnp.ones((HEAD_DIM, tm), BF16)
    for g in range(N_KV_GROUPS):
        sl = slice(g * HEAD_DIM, (g + 1) * HEAD_DIM)
        kc_ref[g] = k_cmp[:, sl]
        vc_ref[g] = v_cmp[:, sl]
        ks_ref[g, :, 0:HEAD_DIM] = k_sel[:, sl].astype(BF16)
        ks_ref[g, :, HEAD_DIM:] = onehot
        kw_ref[g] = k_win[:, sl].astype(BF16)
        vsT_ref[g, 0:HEAD_DIM, :] = v_selT[sl]
        vsT_ref[g, HEAD_DIM:, :] = ones
        vwT_ref[g, 0:HEAD_DIM, :] = v_winT[sl]
        vwT_ref[g, HEAD_DIM:, :] = ones

    gatesT = jax.nn.sigmoid(_dot(xb, wg_ref[...])).T
    for g in range(N_KV_GROUPS):
        gT_ref[g] = gatesT[g * GATES_PER_GROUP:(g + 1) * GATES_PER_GROUP]


def _in_proj(x2d, wq, wkv, wg, wa, wgt, cos_t, sin_t, conv_w, conv_b, conv_g, conv_beta, *,
             batch, seq, tm):
    T, d_model = x2d.shape
    n_row_tiles = seq // tm
    G = N_KV_GROUPS
    row = lambda i: (i, 0)
    tab = lambda i: (i % n_row_tiles, 0)
    per_group = lambda i: (i // n_row_tiles, 0, i % n_row_tiles, 0)
    per_group_t = lambda i: (i // n_row_tiles, 0, 0, i % n_row_tiles)

    def grp(width, dtype):
        return (jax.ShapeDtypeStruct((batch, G, seq, width), dtype),
                pl.BlockSpec((None, G, tm, width), per_group))

    def grp_t(height, dtype):
        return (jax.ShapeDtypeStruct((batch, G, height, seq), dtype),
                pl.BlockSpec((None, G, height, tm), per_group_t))

    outs = [
        (jax.ShapeDtypeStruct((wq.shape[1], T), BF16),
         pl.BlockSpec((wq.shape[1], tm), lambda i: (0, i))),
        grp(HEAD_DIM, F32), grp(HEAD_DIM, F32),
        grp(HEAD_DIM + SEL_LANES, BF16), grp_t(2 * HEAD_DIM, BF16),
        grp(HEAD_DIM, BF16), grp_t(2 * HEAD_DIM, BF16),
        grp_t(GATES_PER_GROUP, F32),
        (jax.ShapeDtypeStruct((T, wa.shape[1]), BF16), pl.BlockSpec((tm, wa.shape[1]), row)),
    ]
    return pl.pallas_call(
        functools.partial(_in_proj_kernel, tm=tm, n_row_tiles=n_row_tiles),
        grid=(T // tm,),
        in_specs=[pl.BlockSpec((tm, d_model), row),
                  _resident(wq.shape), _resident(wkv.shape), _resident(wg.shape),
                  _resident(wa.shape), _resident(wgt.shape),
                  pl.BlockSpec((tm, LANES), tab), pl.BlockSpec((tm, LANES), tab),
                  _resident(conv_w.shape), _resident(conv_b.shape),
                  _resident(conv_g.shape), _resident(conv_beta.shape)],
        out_specs=[o[1] for o in outs],
        out_shape=[o[0] for o in outs],
        scratch_shapes=[pltpu.VMEM((CONV_HALO + tm, wa.shape[1]), F32)],
        compiler_params=pltpu.CompilerParams(dimension_semantics=("arbitrary",),
                                             vmem_limit_bytes=VMEM_LIMIT_BYTES),
        name="in_proj",
    )(x2d, wq, wkv, wg, wa, wgt, cos_t, sin_t, conv_w, conv_b, conv_g, conv_beta)


def _compress_kernel(c_ref, pe_ref, w1a_ref, w1b_ref, w2_ref, w2s_ref, cos_ref, sin_ref,
                     o_ref, *, rotate):
    n_chunks = c_ref.shape[0] // CMP_STRIDE
    c = jnp.concatenate([c_ref[pl.ds(l, n_chunks, stride=CMP_STRIDE), :]
                         for l in range(CMP_STRIDE)], axis=1)
    a = _dot_split(c + pe_ref[0:1, :], w1a_ref[...])
    b = _dot_split(c + pe_ref[1:2, :], w1b_ref[...])
    hid = jax.nn.gelu(a + pltpu.roll(b, n_chunks - 1, 0))
    out = _dot_split(hid, w2_ref[...])
    if rotate:
        out = out * cos_ref[...] + _dot_split(hid, w2s_ref[...]) * sin_ref[...]
        o_ref[...] = out[:, :HEAD_DIM].astype(o_ref.dtype)
    else:
        o_ref[...] = out.T[:HEAD_DIM].astype(o_ref.dtype)


def _compress(tokens, pe2, w1a, w1b, w2, w2s, cos_c, sin_c, *, rotate):
    BG, seq, _ = tokens.shape
    n_chunks = seq // CMP_STRIDE
    out_block = (None, n_chunks, HEAD_DIM) if rotate else (None, HEAD_DIM, n_chunks)
    return pl.pallas_call(
        functools.partial(_compress_kernel, rotate=rotate),
        grid=(BG,),
        in_specs=[pl.BlockSpec((None, seq, HEAD_DIM), lambda i: (i, 0, 0)),
                  _resident(pe2.shape), _resident(w1a.shape), _resident(w1b.shape),
                  _resident(w2.shape), _resident(w2s.shape),
                  _resident(cos_c.shape), _resident(sin_c.shape)],
        out_specs=pl.BlockSpec(out_block, lambda i: (i, 0, 0)),
        out_shape=jax.ShapeDtypeStruct((BG,) + out_block[1:], BF16),
        compiler_params=_cparams(1),
        name="compress_k" if rotate else "compress_v",
    )(tokens, pe2, w1a, w1b, w2, w2s, cos_c, sin_c)


def _heads_to_lanes(qT):
    return jnp.concatenate(
        [qT[h * HEAD_DIM:(h + 1) * HEAD_DIM] for h in range(HEADS_PER_GROUP)], axis=1)


def _tile_heads(a):
    return jnp.concatenate([a] * HEADS_PER_GROUP, axis=1)


def _window_attention(qT, s0, kw_ref, vwT_ref):
    wq = WIN_Q_BLOCK
    d = (lax.broadcasted_iota(jnp.int32, (wq, wq), 0)
         - lax.broadcasted_iota(jnp.int32, (wq, wq), 1))
    subs = []
    for sub in range(qT.shape[1] // wq):
        sq = s0 + sub * wq
        q_sub = _heads_to_lanes(qT[:, sub * wq:(sub + 1) * wq])
        wstart = pl.multiple_of(jnp.maximum(sq - WINDOW, 0), wq)
        sw = []
        for c in range(WINDOW // wq + 1):
            off = sq - wstart - c * wq
            wbias = jnp.where((d <= off) & (d > off - WINDOW), 0.0, NEG_INF)
            sw.append(_dot(kw_ref[pl.ds(wstart + c * wq, wq), :], q_sub) + _tile_heads(wbias))
        sw = jnp.concatenate(sw, axis=0)
        pw = jnp.exp2(sw - jnp.max(sw, axis=0, keepdims=True)).astype(BF16)
        accw = _dot(vwT_ref[:, pl.ds(wstart, WINDOW + wq)], pw)
        subs.append(accw[:HEAD_DIM] * (1.0 / accw[HEAD_DIM:HEAD_DIM + 1]))
    return [jnp.concatenate([o[:, h * wq:(h + 1) * wq] for o in subs], axis=1)
            for h in range(HEADS_PER_GROUP)]


def _cmp_attn_kernel(qT_ref, kc_ref, vcT_ref, ovT_ref, ocT_ref, sbT_ref, *, n_top):
    qblk = CMP_Q_BLOCK
    qb = pl.program_id(1)
    s0 = qb * qblk
    cols = HEADS_PER_GROUP * qblk
    n_cmp = kc_ref.shape[0]
    q_stackT = _heads_to_lanes(qT_ref[...])
    t_lane = s0 + (lax.broadcasted_iota(jnp.int32, (1, cols), 1) & (qblk - 1))

    def attend(n):
        base = (lax.broadcasted_iota(jnp.int32, (n, qblk), 0) * CMP_STRIDE
                - lax.broadcasted_iota(jnp.int32, (n, qblk), 1))
        bias = jnp.where(base <= s0 - (CMP_BLOCK - 1), 0.0, NEG_INF)
        s = _dot(kc_ref[0:n, :], q_stackT) + _tile_heads(bias)
        m = jnp.max(s, axis=0, keepdims=True)
        e = jnp.exp2(s - m)
        den = jnp.sum(e, axis=0, keepdims=True)
        p = e * jnp.where(t_lane >= CMP_BLOCK - 1, 1.0 / den, 0.0)
        o = _dot(vcT_ref[:, 0:n], p.astype(BF16))
        for h in range(HEADS_PER_GROUP):
            ocT_ref[h] = o[:, h * qblk:(h + 1) * qblk]
        p_sum = p[:, 0:qblk]
        for h in range(1, HEADS_PER_GROUP):
            p_sum = p_sum + p[:, h * qblk:(h + 1) * qblk]
        n_sel = n * CMP_STRIDE // SEL_BLOCK
        imp = _dot_f32(ovT_ref[0:n_sel, 0:n], p_sum)

        tq = s0 + lax.broadcasted_iota(jnp.int32, (n_sel, qblk), 1)
        blk = lax.broadcasted_iota(jnp.int32, (n_sel, qblk), 0)
        cur = tq // SEL_BLOCK
        forced = (blk == 0) | (blk == cur) | (blk == cur - 1)
        v = jnp.where(forced, -jnp.inf, jnp.where(blk * SEL_BLOCK <= tq, imp, -1.0))
        idx = blk.astype(F32)
        sel = forced
        for _ in range(n_top - 3):
            vmax = jnp.max(v, axis=0, keepdims=True)
            first = jnp.min(jnp.where(v == vmax, idx, float(SEL_LANES)), axis=0, keepdims=True)
            pick = idx == first
            sel = sel | pick
            v = jnp.where(pick, -jnp.inf, v)
        sbT_ref[0:n_sel, :] = jnp.where(sel, 0.0, MASK_BIAS).astype(BF16)
        if n_sel < SEL_LANES:
            sbT_ref[n_sel:, :] = jnp.full((SEL_LANES - n_sel, qblk), MASK_BIAS, BF16)

    chunks_needed = pl.cdiv((qb + 1) * (qblk // CMP_STRIDE), CMP_ROW_CHUNK)
    for k in range(1, n_cmp // CMP_ROW_CHUNK + 1):
        pl.when(chunks_needed == k)(functools.partial(attend, k * CMP_ROW_CHUNK))


def _cmp_attn(qT, kc, vcT, overlapT, *, batch, seq, n_top):
    BG, n_cmp, _ = kc.shape
    G = N_KV_GROUPS
    qblk = CMP_Q_BLOCK
    nqb = seq // qblk
    return pl.pallas_call(
        functools.partial(_cmp_attn_kernel, n_top=n_top),
        grid=(BG, nqb),
        in_specs=[pl.BlockSpec((GROUP_WIDTH, qblk),
                               lambda bg, qb: (bg % G, (bg // G) * nqb + qb)),
                  pl.BlockSpec((None, n_cmp, HEAD_DIM), lambda bg, qb: (bg, 0, 0)),
                  pl.BlockSpec((None, HEAD_DIM, n_cmp), lambda bg, qb: (bg, 0, 0)),
                  _resident(overlapT.shape)],
        out_specs=[pl.BlockSpec((None, HEADS_PER_GROUP, HEAD_DIM, qblk),
                                lambda bg, qb: (bg, 0, 0, qb)),
                   pl.BlockSpec((None, SEL_LANES, qblk), lambda bg, qb: (bg, 0, qb))],
        out_shape=[jax.ShapeDtypeStruct((BG, HEADS_PER_GROUP, HEAD_DIM, seq), F32),
                   jax.ShapeDtypeStruct((BG, SEL_LANES, seq), BF16)],
        compiler_params=_cparams(2),
        name="cmp_attn_select",
    )(qT, kc, vcT, overlapT)


def _sel_win_kernel(qT_ref, sbT_ref, ks_ref, vsT_ref, kw_ref, vwT_ref, ocT_ref, gT_ref, o_ref,
                    s_ref, p_ref):
    qblk = SEL_Q_BLOCK
    s0 = pl.program_id(1) * qblk
    cols = HEADS_PER_GROUP * qblk
    qT = qT_ref[...]
    biasT = sbT_ref[...]
    q_augT = jnp.concatenate(
        [jnp.concatenate([qT[h * HEAD_DIM:(h + 1) * HEAD_DIM], biasT], axis=0)
         for h in range(HEADS_PER_GROUP)], axis=1)

    def tile_start(kt):
        return pl.multiple_of(kt * qblk, qblk)

    def scores(kt):
        return _dot(ks_ref[pl.ds(tile_start(kt), qblk), :], q_augT)

    def pv(kt, p):
        return _dot(vsT_ref[:, pl.ds(tile_start(kt), qblk)], p)

    def softmax_step(s, kt, m, masked):
        if masked:
            d = (lax.broadcasted_iota(jnp.int32, (qblk, qblk), 0)
                 - lax.broadcasted_iota(jnp.int32, (qblk, qblk), 1))
            s = s + _tile_heads(jnp.where(d <= s0 - kt * qblk, 0.0, MASK_BIAS))
        m_new = jnp.maximum(m, jnp.max(s, axis=0, keepdims=True))
        return m_new, jnp.exp2(m - m_new), jnp.exp2(s - m_new).astype(BF16)

    def pair(i, carry):
        m, acc = carry
        pv_prev = pv(jnp.maximum(2 * i - 1, 0), p_ref[...])
        s_odd = scores(2 * i + 1)
        m, alpha, p = softmax_step(s_ref[...], 2 * i, m, False)
        acc = alpha * (acc + pv_prev) + pv(2 * i, p)
        s_ref[...] = scores(2 * i + 2)
        m, alpha, p = softmax_step(s_odd, 2 * i + 1, m, False)
        p_ref[...] = p
        return m, alpha * acc

    def tail(kt, n_full, carry):
        m, acc = carry
        pv_prev = pv(jnp.maximum(kt - 1, 0), p_ref[...])
        m, alpha, p = softmax_step(s_ref[...], kt, m, True)
        acc = alpha * (acc + pv_prev) + pv(kt, p)

        def diagonal(c):
            m, acc = c
            m, alpha, p = softmax_step(scores(kt + 1), kt + 1, m, True)
            return m, alpha * acc + pv(kt + 1, p)

        return lax.cond(kt < n_full, diagonal, lambda c: c, (m, acc))

    s_ref[...] = scores(0)
    p_ref[...] = jnp.zeros_like(p_ref)
    o_w = _window_attention(qT, s0, kw_ref, vwT_ref)
    n_full = pl.program_id(1)
    n_main = n_full // 2
    init = (jnp.full((1, cols), NEG_INF, F32), jnp.zeros((2 * HEAD_DIM, cols), F32))
    carry = lax.fori_loop(0, n_main, pair, init)
    _, acc = tail(2 * n_main, n_full, carry)
    o_s = acc[:HEAD_DIM] * (1.0 / acc[HEAD_DIM:HEAD_DIM + 1])

    gT = gT_ref[...]
    outs = []
    for h in range(HEADS_PER_GROUP):
        c = slice(h * qblk, (h + 1) * qblk)
        gate = lambda br: gT[h * N_BRANCH + br:h * N_BRANCH + br + 1, :]
        outs.append(gate(0) * ocT_ref[h] + gate(1) * o_s[:, c] + gate(2) * o_w[h])
    o_ref[...] = jnp.concatenate(outs, axis=0).T.astype(o_ref.dtype)


def _sel_win_attn(qT, sel_biasT, ks_aug, vsT_aug, kw, vwT_aug, o_cmpT, gatesT, *,
                  batch, seq):
    G = N_KV_GROUPS
    BG = batch * G
    qblk = SEL_Q_BLOCK
    nqb = seq // qblk
    whole = lambda a: pl.BlockSpec((None,) + a.shape[1:], lambda bg, qb: (bg, 0, 0),
                                   pipeline_mode=pl.Buffered(1))
    return pl.pallas_call(
        _sel_win_kernel,
        grid=(BG, nqb),
        in_specs=[pl.BlockSpec((GROUP_WIDTH, qblk),
                               lambda bg, qb: (bg % G, (bg // G) * nqb + qb)),
                  pl.BlockSpec((None, SEL_LANES, qblk), lambda bg, qb: (bg, 0, qb)),
                  whole(ks_aug), whole(vsT_aug), whole(kw), whole(vwT_aug),
                  pl.BlockSpec((None, HEADS_PER_GROUP, HEAD_DIM, qblk),
                               lambda bg, qb: (bg, 0, 0, qb)),
                  pl.BlockSpec((None, GATES_PER_GROUP, qblk), lambda bg, qb: (bg, 0, qb))],
        out_specs=pl.BlockSpec((qblk, GROUP_WIDTH),
                               lambda bg, qb: ((bg // G) * nqb + qb, bg % G)),
        out_shape=jax.ShapeDtypeStruct((batch * seq, N_Q_HEADS * HEAD_DIM), BF16),
        scratch_shapes=[pltpu.VMEM((qblk, HEADS_PER_GROUP * qblk), F32),
                        pltpu.VMEM((qblk, HEADS_PER_GROUP * qblk), BF16)],
        compiler_params=_cparams(2),
        name="sel_win_attn",
    )(qT, sel_biasT, ks_aug, vsT_aug, kw, vwT_aug, o_cmpT, gatesT)


def _out_proj_kernel(a_ref, c_ref, x_ref, wa_ref, wc_ref, g_ref, b_ref, o_ref, *, alpha):
    mix = _dot(a_ref[...], wa_ref[...]) + _dot(c_ref[...], wc_ref[...])
    o_ref[...] = _layer_norm(alpha * x_ref[...] + mix, g_ref[...], b_ref[...])


def _out_proj(attn, conv, x2d, w_attn, w_conv, g, b, *, alpha, tm):
    T, d_model = x2d.shape
    row = lambda i: (i, 0)
    return pl.pallas_call(
        functools.partial(_out_proj_kernel, alpha=alpha),
        grid=(T // tm,),
        in_specs=[pl.BlockSpec((tm, attn.shape[1]), row), pl.BlockSpec((tm, conv.shape[1]), row),
                  pl.BlockSpec((tm, d_model), row),
                  _resident(w_attn.shape), _resident(w_conv.shape),
                  _resident(g.shape), _resident(b.shape)],
        out_specs=pl.BlockSpec((tm, d_model), row),
        out_shape=jax.ShapeDtypeStruct((T, d_model), F32),
        compiler_params=_cparams(1),
        name="out_proj_deepnorm",
    )(attn, conv, x2d, w_attn, w_conv, g, b)


def _ffn_kernel(h_ref, w1_ref, w2_ref, g_ref, b_ref, o_ref, hb_ref, *, alpha):
    j = pl.program_id(1)

    @pl.when(j == 0)
    def _():
        hb_ref[...] = h_ref[...].astype(BF16)
        o_ref[...] = jnp.zeros_like(o_ref)

    a = jnp.maximum(_dot(hb_ref[...], w1_ref[...]), 0.0)
    o_ref[...] += _dot((a * a).astype(BF16), w2_ref[...])

    @pl.when(j == pl.num_programs(1) - 1)
    def _():
        o_ref[...] = _layer_norm(alpha * h_ref[...] + o_ref[...], g_ref[...], b_ref[...])


def _ffn(h, w1, w2, g, b, *, alpha, tm, tf):
    T, d_model = h.shape
    d_ff = w1.shape[1]
    return pl.pallas_call(
        functools.partial(_ffn_kernel, alpha=alpha),
        grid=(T // tm, d_ff // tf),
        in_specs=[pl.BlockSpec((tm, d_model), lambda i, j: (i, 0)),
                  pl.BlockSpec((d_model, tf), lambda i, j: (0, j)),
                  pl.BlockSpec((tf, d_model), lambda i, j: (j, 0)),
                  _resident(g.shape), _resident(b.shape)],
        out_specs=pl.BlockSpec((tm, d_model), lambda i, j: (i, 0)),
        out_shape=jax.ShapeDtypeStruct((T, d_model), F32),
        scratch_shapes=[pltpu.VMEM((tm, d_model), BF16)],
        compiler_params=pltpu.CompilerParams(dimension_semantics=("parallel", "arbitrary"),
                                             vmem_limit_bytes=VMEM_LIMIT_BYTES),
        name="ffn_deepnorm",
    )(h, w1, w2, g, b)


def _rope_tables(pos, reps):
    half = HEAD_DIM // 2
    inv_freq = ROPE_THETA ** (-jnp.arange(half, dtype=F32) / half)
    ang = pos.astype(F32)[:, None] * inv_freq[None, :]
    cos, sin = jnp.cos(ang), jnp.sin(ang)
    return jnp.tile(jnp.concatenate([cos, cos], axis=1), (1, reps)), \
        jnp.tile(jnp.concatenate([-sin, sin], axis=1), (1, reps))


def _tiles(seq):
    return dict(in_proj=min(512, seq), out_proj=min(512, seq),
                ffn_rows=min(512, seq), ffn_cols=1024)


def _mixer(x2d, w_in, cmp_pe_k, cmp_w1_k, cmp_w2_k, cmp_pe_v, cmp_w1_v, cmp_w2_v,
           conv_w, conv_b, conv_ln_g, conv_ln_b, *, batch, seq):
    tiles = _tiles(seq)
    assert seq % SEL_Q_BLOCK == 0 and seq % CMP_Q_BLOCK == 0, seq
    assert seq >= WINDOW + WIN_Q_BLOCK, seq
    assert (seq // CMP_STRIDE) % CMP_ROW_CHUNK == 0, seq
    assert 3 <= min(N_SEL, seq // SEL_BLOCK) and seq // SEL_BLOCK <= SEL_LANES, seq
    attn_w = N_Q_HEADS * HEAD_DIM
    kv_end = attn_w + 6 * KV_WIDTH
    gate_end = kv_end + N_Q_HEADS * N_BRANCH
    conv_width = (w_in.shape[1] - gate_end) // 2
    wq = w_in[:, :attn_w].astype(BF16)
    wkv = w_in[:, attn_w:kv_end].astype(BF16)
    wg = jnp.pad(w_in[:, kv_end:gate_end], ((0, 0), (0, LANES - (gate_end - kv_end)))).astype(BF16)
    wa = w_in[:, gate_end:gate_end + conv_width].astype(BF16)
    wgt = w_in[:, gate_end + conv_width:].astype(BF16)
    cos_t, sin_t = _rope_tables(jnp.arange(seq), 2)

    qT, k_cmp, v_cmp, ks_aug, vsT_aug, k_win, vwT_aug, gatesT, conv = _in_proj(
        x2d, wq, wkv, wg, wa, wgt, cos_t, sin_t,
        conv_w.reshape(CONV_KERNEL, conv_width), conv_b.reshape(1, -1),
        conv_ln_g.reshape(1, -1), conv_ln_b.reshape(1, -1),
        batch=batch, seq=seq, tm=tiles["in_proj"])

    BG = batch * N_KV_GROUPS
    n_chunks = seq // CMP_STRIDE
    chunk_w = CMP_STRIDE * HEAD_DIM
    cmp_end = jnp.arange(n_chunks) * CMP_STRIDE + CMP_BLOCK - 1
    cos_c, sin_c = _rope_tables(cmp_end, 2)
    pad_cols = lambda w2: jnp.pad(w2, ((0, 0), (0, LANES - HEAD_DIM)))
    swap = lambda w2: jnp.concatenate([w2[:, HEAD_DIM // 2:], w2[:, :HEAD_DIM // 2]], axis=1)

    def compress(t, pe, w1, w2, rotate):
        return _compress(t.reshape(BG, seq, HEAD_DIM), pe.reshape(2, chunk_w),
                         w1[:CMP_STRIDE].reshape(chunk_w, -1), w1[CMP_STRIDE:].reshape(chunk_w, -1),
                         pad_cols(w2), pad_cols(swap(w2)), cos_c, sin_c, rotate=rotate)

    kc = compress(k_cmp, cmp_pe_k, cmp_w1_k, cmp_w2_k, True)
    vcT = compress(v_cmp, cmp_pe_v, cmp_w1_v, cmp_w2_v, False)

    n_slc = seq // SEL_BLOCK
    cmp_start = jnp.arange(n_chunks) * CMP_STRIDE
    slc_start = jnp.arange(SEL_LANES) * SEL_BLOCK
    overlapT = ((cmp_start[None, :] < slc_start[:, None] + SEL_BLOCK)
                & (cmp_start[None, :] + CMP_BLOCK - 1 >= slc_start[:, None])
                & (jnp.arange(n_chunks)[None, :] < n_chunks - 1)
                & (jnp.arange(SEL_LANES)[:, None] < n_slc)).astype(F32)
    o_cmpT, sel_biasT = _cmp_attn(qT, kc, vcT, overlapT, batch=batch, seq=seq,
                                  n_top=min(N_SEL, n_slc))

    flat = lambda a: a.reshape((BG,) + a.shape[2:])
    attn = _sel_win_attn(qT, sel_biasT, flat(ks_aug), flat(vsT_aug), flat(k_win), flat(vwT_aug),
                         o_cmpT, flat(gatesT), batch=batch, seq=seq)
    return attn, conv


def kernel(x, w_in, cmp_pe_k, cmp_w1_k, cmp_w2_k, cmp_pe_v, cmp_w1_v, cmp_w2_v, conv_w, conv_b,
           conv_ln_g, conv_ln_b, w_out, ln1_g, ln1_b, w_ff1, w_ff2, ln2_g, ln2_b):
    batch, seq, d_model = x.shape
    depth = w_in.shape[0]
    alpha = (2.0 * depth) ** 0.25
    tiles = _tiles(seq)
    attn_w = N_Q_HEADS * HEAD_DIM
    h = x.reshape(batch * seq, d_model)
    for l in range(depth):
        attn, conv = _mixer(h, w_in[l], cmp_pe_k[l], cmp_w1_k[l], cmp_w2_k[l],
                            cmp_pe_v[l], cmp_w1_v[l], cmp_w2_v[l],
                            conv_w[l], conv_b[l], conv_ln_g[l], conv_ln_b[l],
                            batch=batch, seq=seq)
        h = _out_proj(attn, conv, h, w_out[l, :attn_w].astype(BF16), w_out[l, attn_w:].astype(BF16),
                      ln1_g[l].reshape(1, -1), ln1_b[l].reshape(1, -1),
                      alpha=alpha, tm=tiles["out_proj"])
        h = _ffn(h, w_ff1[l].astype(BF16), w_ff2[l].astype(BF16),
                 ln2_g[l].reshape(1, -1), ln2_b[l].reshape(1, -1),
                 alpha=alpha, tm=tiles["ffn_rows"], tf=tiles["ffn_cols"])
    return h.reshape(batch, seq, d_model)
```

```python
import functools
import math

import jax
import jax.numpy as jnp
from jax import lax
from jax.experimental import pallas as pl
from jax.experimental.pallas import tpu as pltpu

F32 = jnp.float32
BF16 = jnp.bfloat16

HEAD_DIM = 64
N_KV_GROUPS = 4
HEADS_PER_GROUP = 4
N_Q_HEADS = N_KV_GROUPS * HEADS_PER_GROUP
GROUP_WIDTH = HEADS_PER_GROUP * HEAD_DIM
N_BRANCH = 3
KV_WIDTH = N_KV_GROUPS * HEAD_DIM
CMP_BLOCK = 32
CMP_STRIDE = 16
SEL_BLOCK = 64
N_SEL = 16
WINDOW = 512
NEG_INF = -1.0e30
MASK_BIAS = -(2.0 ** 100)
CONV_KERNEL = 31
CONV_HALO = 32
CONV_ROWS = 128
ROPE_THETA = 10000.0
LN_EPS = 1e-5
LANES = 128
SUBLANES = 8
PROJ_COLS = 512
SEL_LANES = LANES
CMP_Q_BLOCK = 512
SEL_Q_BLOCK = 512
WIN_Q_BLOCK = 256
CMP_ROW_CHUNK = 128
GATES_PER_GROUP = HEADS_PER_GROUP * N_BRANCH
VMEM_LIMIT_BYTES = 56 * 1024 * 1024


def _cparams(n_axes):
    return pltpu.CompilerParams(dimension_semantics=("parallel",) * n_axes,
                                vmem_limit_bytes=VMEM_LIMIT_BYTES)


def _resident(shape):
    nd = len(shape)
    return pl.BlockSpec(shape, lambda *_: (0,) * nd, pipeline_mode=pl.Buffered(1))


def _dot(a, b):
    return jnp.dot(a, b, preferred_element_type=F32)


def _dot_split(a, b):
    a_hi, b_hi = a.astype(BF16), b.astype(BF16)
    a_lo = (a - a_hi.astype(F32)).astype(BF16)
    b_lo = (b - b_hi.astype(F32)).astype(BF16)
    return _dot(a_hi, b_hi) + (_dot(a_hi, b_lo) + _dot(a_lo, b_hi))


def _layer_norm(y, g, b):
    mu = jnp.mean(y, axis=-1, keepdims=True)
    d = y - mu
    var = jnp.mean(d * d, axis=-1, keepdims=True)
    return d * lax.rsqrt(var + LN_EPS) * g + b


def _conv_ln_swish(win_ref, r0, rows, w_ref, b_ref, g_ref, beta_ref):
    span = CONV_HALO + rows
    cols = []
    for c in range(win_ref.shape[1] // LANES):
        sl = slice(c * LANES, (c + 1) * LANES)
        win = win_ref[r0:r0 + span, sl]
        acc = jnp.zeros((rows, LANES), F32)
        for r in range(SUBLANES):
            shifted = win if r == 0 else pltpu.roll(win, span - r, 0)
            for j in range(CONV_KERNEL):
                off = CONV_HALO - (CONV_KERNEL - 1) + j
                if off % SUBLANES == r:
                    acc = acc + shifted[off - r:off - r + rows] * w_ref[j:j + 1, sl]
        cols.append(acc)
    y = jnp.concatenate(cols, axis=1) + b_ref[...]
    y = _layer_norm(y, g_ref[...], beta_ref[...])
    return y * jax.nn.sigmoid(y)


def _in_proj_kernel(x_ref, wq_ref, wkv_ref, wg_ref, wa_ref, wgt_ref, cos_ref, sin_ref,
                    cw_ref, cb_ref, cg_ref, cbeta_ref,
                    qT_ref, kc_ref, vc_ref, ks_ref, vsT_ref, kw_ref, vwT_ref, gT_ref, conv_ref,
                    win_ref, *, tm, n_row_tiles):
    @pl.when(pl.program_id(0) % n_row_tiles == 0)
    def _():
        win_ref[0:CONV_HALO, :] = jnp.zeros((CONV_HALO, win_ref.shape[1]), F32)

    xb = x_ref[...].astype(BF16)
    cos = cos_ref[...]
    sin = sin_ref[...]
    lane = lax.broadcasted_iota(jnp.int32, (tm, LANES), 1)
    first_half = (lane & (HEAD_DIM - 1)) < (HEAD_DIM // 2)

    def rope(t):
        partner = jnp.where(first_half, pltpu.roll(t, LANES - HEAD_DIM // 2, 1),
                            pltpu.roll(t, HEAD_DIM // 2, 1))
        return t * cos + partner * sin

    for c in range(wa_ref.shape[1] // PROJ_COLS):
        sl = slice(c * PROJ_COLS, (c + 1) * PROJ_COLS)
        a = _dot(xb, wa_ref[:, sl])
        gt = _dot(xb, wgt_ref[:, sl])
        win_ref[CONV_HALO:, sl] = a * jax.nn.sigmoid(gt)
    for r0 in range(0, tm, CONV_ROWS):
        conv_ref[r0:r0 + CONV_ROWS, :] = _conv_ln_swish(
            win_ref, r0, CONV_ROWS, cw_ref, cb_ref, cg_ref, cbeta_ref).astype(conv_ref.dtype)
    win_ref[0:CONV_HALO, :] = win_ref[tm:tm + CONV_HALO, :]

    q_scale = HEAD_DIM ** -0.5 * math.log2(math.e)
    for c in range(wq_ref.shape[1] // PROJ_COLS):
        sl = slice(c * PROJ_COLS, (c + 1) * PROJ_COLS)
        t = _dot(xb, wq_ref[:, sl])
        t = jnp.concatenate([rope(t[:, cc * LANES:(cc + 1) * LANES])
                             for cc in range(PROJ_COLS // LANES)], axis=1)
        qT_ref[sl, :] = (t * q_scale).T.astype(BF16)

    kv = [_dot(xb, wkv_ref[:, j * KV_WIDTH:(j + 1) * KV_WIDTH]) for j in range(6)]
    k_cmp, v_cmp, k_sel, v_sel, k_win, v_win = kv
    k_sel = jnp.concatenate([rope(k_sel[:, :LANES]), rope(k_sel[:, LANES:])], axis=1)
    k_win = jnp.concatenate([rope(k_win[:, :LANES]), rope(k_win[:, LANES:])], axis=1)
    v_selT = v_sel.T.astype(BF16)
    v_winT = v_win.T.astype(BF16)

    s_start = (pl.program_id(0) % n_row_tiles) * tm
    key_blk = (s_start + lax.broadcasted_iota(jnp.int32, (tm, SEL_LANES), 0)) // SEL_BLOCK
    onehot = (key_blk == lax.broadcasted_iota(jnp.int32, (tm, SEL_LANES), 1)).astype(BF16)
    ones = jnp.ones((HEAD_DIM, tm), BF16)
    for g in range(N_KV_GROUPS):
        sl = slice(g * HEAD_DIM, (g + 1) * HEAD_DIM)
        kc_ref[g] = k_cmp[:, sl]
        vc_ref[g] = v_cmp[:, sl]
        ks_ref[g, :, 0:HEAD_DIM] = k_sel[:, sl].astype(BF16)
        ks_ref[g, :, HEAD_DIM:] = onehot
        kw_ref[g] = k_win[:, sl].astype(BF16)
        vsT_ref[g, 0:HEAD_DIM, :] = v_selT[sl]
        vsT_ref[g, HEAD_DIM:, :] = ones
        vwT_ref[g, 0:HEAD_DIM, :] = v_winT[sl]
        vwT_ref[g, HEAD_DIM:, :] = ones

    gatesT = jax.nn.sigmoid(_dot(xb, wg_ref[...])).T
    for g in range(N_KV_GROUPS):
        gT_ref[g] = gatesT[g * GATES_PER_GROUP:(g + 1) * GATES_PER_GROUP]


def _in_proj(x2d, wq, wkv, wg, wa, wgt, cos_t, sin_t, conv_w, conv_b, conv_g, conv_beta, *,
             batch, seq, tm):
    T, d_model = x2d.shape
    n_row_tiles = seq // tm
    G = N_KV_GROUPS
    row = lambda i: (i, 0)
    tab = lambda i: (i % n_row_tiles, 0)
    per_group = lambda i: (i // n_row_tiles, 0, i % n_row_tiles, 0)
    per_group_t = lambda i: (i // n_row_tiles, 0, 0, i % n_row_tiles)

    def grp(width, dtype):
        return (jax.ShapeDtypeStruct((batch, G, seq, width), dtype),
                pl.BlockSpec((None, G, tm, width), per_group))

    def grp_t(height, dtype):
        return (jax.ShapeDtypeStruct((batch, G, height, seq), dtype),
                pl.BlockSpec((None, G, height, tm), per_group_t))

    outs = [
        (jax.ShapeDtypeStruct((wq.shape[1], T), BF16),
         pl.BlockSpec((wq.shape[1], tm), lambda i: (0, i))),
        grp(HEAD_DIM, F32), grp(HEAD_DIM, F32),
        grp(HEAD_DIM + SEL_LANES, BF16), grp_t(2 * HEAD_DIM, BF16),
        grp(HEAD_DIM, BF16), grp_t(2 * HEAD_DIM, BF16),
        grp_t(GATES_PER_GROUP, F32),
        (jax.ShapeDtypeStruct((T, wa.shape[1]), BF16), pl.BlockSpec((tm, wa.shape[1]), row)),
    ]
    return pl.pallas_call(
        functools.partial(_in_proj_kernel, tm=tm, n_row_tiles=n_row_tiles),
        grid=(T // tm,),
        in_specs=[pl.BlockSpec((tm, d_model), row),
                  _resident(wq.shape), _resident(wkv.shape), _resident(wg.shape),
                  _resident(wa.shape), _resident(wgt.shape),
                  pl.BlockSpec((tm, LANES), tab), pl.BlockSpec((tm, LANES), tab),
                  _resident(conv_w.shape), _resident(conv_b.shape),
                  _resident(conv_g.shape), _resident(conv_beta.shape)],
        out_specs=[o[1] for o in outs],
        out_shape=[o[0] for o in outs],
        scratch_shapes=[pltpu.VMEM((CONV_HALO + tm, wa.shape[1]), F32)],
        compiler_params=pltpu.CompilerParams(dimension_semantics=("arbitrary",),
                                             vmem_limit_bytes=VMEM_LIMIT_BYTES),
        name="in_proj",
    )(x2d, wq, wkv, wg, wa, wgt, cos_t, sin_t, conv_w, conv_b, conv_g, conv_beta)


def _compress_kernel(c_ref, pe_ref, w1a_ref, w1b_ref, w2_ref, w2s_ref, cos_ref, sin_ref,
                     o_ref, *, rotate):
    n_chunks = c_ref.shape[0] // CMP_STRIDE
    c = jnp.concatenate([c_ref[pl.ds(l, n_chunks, stride=CMP_STRIDE), :]
                         for l in range(CMP_STRIDE)], axis=1)
    a = _dot_split(c + pe_ref[0:1, :], w1a_ref[...])
    b = _dot_split(c + pe_ref[1:2, :], w1b_ref[...])
    hid = jax.nn.gelu(a + pltpu.roll(b, n_chunks - 1, 0))
    out = _dot_split(hid, w2_ref[...])
    if rotate:
        out = out * cos_ref[...] + _dot_split(hid, w2s_ref[...]) * sin_ref[...]
        o_ref[...] = out[:, :HEAD_DIM].astype(o_ref.dtype)
    else:
        o_ref[...] = out.T[:HEAD_DIM].astype(o_ref.dtype)


def _compress(tokens, pe2, w1a, w1b, w2, w2s, cos_c, sin_c, *, rotate):
    BG, seq, _ = tokens.shape
    n_chunks = seq // CMP_STRIDE
    out_block = (None, n_chunks, HEAD_DIM) if rotate else (None, HEAD_DIM, n_chunks)
    return pl.pallas_call(
        functools.partial(_compress_kernel, rotate=rotate),
        grid=(BG,),
        in_specs=[pl.BlockSpec((None, seq, HEAD_DIM), lambda i: (i, 0, 0)),
                  _resident(pe2.shape), _resident(w1a.shape), _resident(w1b.shape),
                  _resident(w2.shape), _resident(w2s.shape),
                  _resident(cos_c.shape), _resident(sin_c.shape)],
        out_specs=pl.BlockSpec(out_block, lambda i: (i, 0, 0)),
        out_shape=jax.ShapeDtypeStruct((BG,) + out_block[1:], BF16),
        compiler_params=_cparams(1),
        name="compress_k" if rotate else "compress_v",
    )(tokens, pe2, w1a, w1b, w2, w2s, cos_c, sin_c)


def _heads_to_lanes(qT):
    return jnp.concatenate(
        [qT[h * HEAD_DIM:(h + 1) * HEAD_DIM] for h in range(HEADS_PER_GROUP)], axis=1)


def _tile_heads(a):
    return jnp.concatenate([a] * HEADS_PER_GROUP, axis=1)


def _window_attention(qT, s0, kw_ref, vwT_ref):
    wq = WIN_Q_BLOCK
    d = (lax.broadcasted_iota(jnp.int32, (wq, wq), 0)
         - lax.broadcasted_iota(jnp.int32, (wq, wq), 1))
    subs = []
    for sub in range(qT.shape[1] // wq):
        sq = s0 + sub * wq
        q_sub = _heads_to_lanes(qT[:, sub * wq:(sub + 1) * wq])
        wstart = pl.multiple_of(jnp.maximum(sq - WINDOW, 0), wq)
        sw = []
        for c in range(WINDOW // wq + 1):
            off = sq - wstart - c * wq
            wbias = jnp.where((d <= off) & (d > off - WINDOW), 0.0, NEG_INF)
            sw.append(_dot(kw_ref[pl.ds(wstart + c * wq, wq), :], q_sub) + _tile_heads(wbias))
        sw = jnp.concatenate(sw, axis=0)
        pw = jnp.exp2(sw - jnp.max(sw, axis=0, keepdims=True)).astype(BF16)
        accw = _dot(vwT_ref[:, pl.ds(wstart, WINDOW + wq)], pw)
        subs.append(accw[:HEAD_DIM] * (1.0 / accw[HEAD_DIM:HEAD_DIM + 1]))
    return [jnp.concatenate([o[:, h * wq:(h + 1) * wq] for o in subs], axis=1)
            for h in range(HEADS_PER_GROUP)]


def _cmp_attn_kernel(qT_ref, kc_ref, vcT_ref, ovT_ref, ocT_ref, sbT_ref, *, n_top):
    qblk = CMP_Q_BLOCK
    qb = pl.program_id(1)
    s0 = qb * qblk
    cols = HEADS_PER_GROUP * qblk
    n_cmp = kc_ref.shape[0]
    q_stackT = _heads_to_lanes(qT_ref[...])
    t_lane = s0 + (lax.broadcasted_iota(jnp.int32, (1, cols), 1) & (qblk - 1))

    def attend(n):
        base = (lax.broadcasted_iota(jnp.int32, (n, qblk), 0) * CMP_STRIDE
                - lax.broadcasted_iota(jnp.int32, (n, qblk), 1))
        bias = jnp.where(base <= s0 - (CMP_BLOCK - 1), 0.0, NEG_INF)
        s = _dot(kc_ref[0:n, :], q_stackT) + _tile_heads(bias)
        m = jnp.max(s, axis=0, keepdims=True)
        e = jnp.exp2(s - m)
        den = jnp.sum(e, axis=0, keepdims=True)
        p = e * jnp.where(t_lane >= CMP_BLOCK - 1, 1.0 / den, 0.0)
        o = _dot(vcT_ref[:, 0:n], p.astype(BF16))
        for h in range(HEADS_PER_GROUP):
            ocT_ref[h] = o[:, h * qblk:(h + 1) * qblk]
        p_sum = p[:, 0:qblk]
        for h in range(1, HEADS_PER_GROUP):
            p_sum = p_sum + p[:, h * qblk:(h + 1) * qblk]
        n_sel = n * CMP_STRIDE // SEL_BLOCK
        imp = _dot_split(ovT_ref[0:n_sel, 0:n], p_sum)

        tq = s0 + lax.broadcasted_iota(jnp.int32, (n_sel, qblk), 1)
        blk = lax.broadcasted_iota(jnp.int32, (n_sel, qblk), 0)
        cur = tq // SEL_BLOCK
        forced = (blk == 0) | (blk == cur) | (blk == cur - 1)
        v = jnp.where(forced, -jnp.inf, jnp.where(blk * SEL_BLOCK <= tq, imp, -1.0))
        idx = blk.astype(F32)
        sel = forced
        for _ in range(n_top - 3):
            vmax = jnp.max(v, axis=0, keepdims=True)
            first = jnp.min(jnp.where(v == vmax, idx, float(SEL_LANES)), axis=0, keepdims=True)
            pick = idx == first
            sel = sel | pick
            v = jnp.where(pick, -jnp.inf, v)
        sbT_ref[0:n_sel, :] = jnp.where(sel, 0.0, MASK_BIAS).astype(BF16)
        if n_sel < SEL_LANES:
            sbT_ref[n_sel:, :] = jnp.full((SEL_LANES - n_sel, qblk), MASK_BIAS, BF16)

    chunks_needed = pl.cdiv((qb + 1) * (qblk // CMP_STRIDE), CMP_ROW_CHUNK)
    for k in range(1, n_cmp // CMP_ROW_CHUNK + 1):
        pl.when(chunks_needed == k)(functools.partial(attend, k * CMP_ROW_CHUNK))


def _cmp_attn(qT, kc, vcT, overlapT, *, batch, seq, n_top):
    BG, n_cmp, _ = kc.shape
    G = N_KV_GROUPS
    qblk = CMP_Q_BLOCK
    nqb = seq // qblk
    return pl.pallas_call(
        functools.partial(_cmp_attn_kernel, n_top=n_top),
        grid=(BG, nqb),
        in_specs=[pl.BlockSpec((GROUP_WIDTH, qblk),
                               lambda bg, qb: (bg % G, (bg // G) * nqb + qb)),
                  pl.BlockSpec((None, n_cmp, HEAD_DIM), lambda bg, qb: (bg, 0, 0)),
                  pl.BlockSpec((None, HEAD_DIM, n_cmp), lambda bg, qb: (bg, 0, 0)),
                  _resident(overlapT.shape)],
        out_specs=[pl.BlockSpec((None, HEADS_PER_GROUP, HEAD_DIM, qblk),
                                lambda bg, qb: (bg, 0, 0, qb)),
                   pl.BlockSpec((None, SEL_LANES, qblk), lambda bg, qb: (bg, 0, qb))],
        out_shape=[jax.ShapeDtypeStruct((BG, HEADS_PER_GROUP, HEAD_DIM, seq), F32),
                   jax.ShapeDtypeStruct((BG, SEL_LANES, seq), BF16)],
        compiler_params=_cparams(2),
        name="cmp_attn_select",
    )(qT, kc, vcT, overlapT)


def _sel_win_kernel(qT_ref, sbT_ref, ks_ref, vsT_ref, kw_ref, vwT_ref, ocT_ref, gT_ref, o_ref,
                    s_ref, p_ref):
    qblk = SEL_Q_BLOCK
    s0 = pl.program_id(1) * qblk
    cols = HEADS_PER_GROUP * qblk
    qT = qT_ref[...]
    biasT = sbT_ref[...]
    q_augT = jnp.concatenate(
        [jnp.concatenate([qT[h * HEAD_DIM:(h + 1) * HEAD_DIM], biasT], axis=0)
         for h in range(HEADS_PER_GROUP)], axis=1)

    def tile_start(kt):
        return pl.multiple_of(kt * qblk, qblk)

    def scores(kt):
        return _dot(ks_ref[pl.ds(tile_start(kt), qblk), :], q_augT)

    def pv(kt, p):
        return _dot(vsT_ref[:, pl.ds(tile_start(kt), qblk)], p)

    def softmax_step(s, kt, m, masked):
        if masked:
            d = (lax.broadcasted_iota(jnp.int32, (qblk, qblk), 0)
                 - lax.broadcasted_iota(jnp.int32, (qblk, qblk), 1))
            s = s + _tile_heads(jnp.where(d <= s0 - kt * qblk, 0.0, MASK_BIAS))
        m_new = jnp.maximum(m, jnp.max(s, axis=0, keepdims=True))
        return m_new, jnp.exp2(m - m_new), jnp.exp2(s - m_new).astype(BF16)

    def pair(i, carry):
        m, acc = carry
        pv_prev = pv(jnp.maximum(2 * i - 1, 0), p_ref[...])
        s_odd = scores(2 * i + 1)
        m, alpha, p = softmax_step(s_ref[...], 2 * i, m, False)
        acc = alpha * (acc + pv_prev) + pv(2 * i, p)
        s_ref[...] = scores(2 * i + 2)
        m, alpha, p = softmax_step(s_odd, 2 * i + 1, m, False)
        p_ref[...] = p
        return m, alpha * acc

    def tail(kt, n_full, carry):
        m, acc = carry
        pv_prev = pv(jnp.maximum(kt - 1, 0), p_ref[...])
        m, alpha, p = softmax_step(s_ref[...], kt, m, True)
        acc = alpha * (acc + pv_prev) + pv(kt, p)

        def diagonal(c):
            m, acc = c
            m, alpha, p = softmax_step(scores(kt + 1), kt + 1, m, True)
            return m, alpha * acc + pv(kt + 1, p)

        return lax.cond(kt < n_full, diagonal, lambda c: c, (m, acc))

    s_ref[...] = scores(0)
    p_ref[...] = jnp.zeros_like(p_ref)
    o_w = _window_attention(qT, s0, kw_ref, vwT_ref)
    n_full = pl.program_id(1)
    n_main = n_full // 2
    init = (jnp.full((1, cols), NEG_INF, F32), jnp.zeros((2 * HEAD_DIM, cols), F32))
    carry = lax.fori_loop(0, n_main, pair, init)
    _, acc = tail(2 * n_main, n_full, carry)
    o_s = acc[:HEAD_DIM] * (1.0 / acc[HEAD_DIM:HEAD_DIM + 1])

    gT = gT_ref[...]
    outs = []
    for h in range(HEADS_PER_GROUP):
        c = slice(h * qblk, (h + 1) * qblk)
        gate = lambda br: gT[h * N_BRANCH + br:h * N_BRANCH + br + 1, :]
        outs.append(gate(0) * ocT_ref[h] + gate(1) * o_s[:, c] + gate(2) * o_w[h])
    o_ref[...] = jnp.concatenate(outs, axis=0).T.astype(o_ref.dtype)


def _sel_win_attn(qT, sel_biasT, ks_aug, vsT_aug, kw, vwT_aug, o_cmpT, gatesT, *,
                  batch, seq):
    G = N_KV_GROUPS
    BG = batch * G
    qblk = SEL_Q_BLOCK
    nqb = seq // qblk
    whole = lambda a: pl.BlockSpec((None,) + a.shape[1:], lambda bg, qb: (bg, 0, 0),
                                   pipeline_mode=pl.Buffered(1))
    return pl.pallas_call(
        _sel_win_kernel,
        grid=(BG, nqb),
        in_specs=[pl.BlockSpec((GROUP_WIDTH, qblk),
                               lambda bg, qb: (bg % G, (bg // G) * nqb + qb)),
                  pl.BlockSpec((None, SEL_LANES, qblk), lambda bg, qb: (bg, 0, qb)),
                  whole(ks_aug), whole(vsT_aug), whole(kw), whole(vwT_aug),
                  pl.BlockSpec((None, HEADS_PER_GROUP, HEAD_DIM, qblk),
                               lambda bg, qb: (bg, 0, 0, qb)),
                  pl.BlockSpec((None, GATES_PER_GROUP, qblk), lambda bg, qb: (bg, 0, qb))],
        out_specs=pl.BlockSpec((qblk, GROUP_WIDTH),
                               lambda bg, qb: ((bg // G) * nqb + qb, bg % G)),
        out_shape=jax.ShapeDtypeStruct((batch * seq, N_Q_HEADS * HEAD_DIM), BF16),
        scratch_shapes=[pltpu.VMEM((qblk, HEADS_PER_GROUP * qblk), F32),
                        pltpu.VMEM((qblk, HEADS_PER_GROUP * qblk), BF16)],
        compiler_params=_cparams(2),
        name="sel_win_attn",
    )(qT, sel_biasT, ks_aug, vsT_aug, kw, vwT_aug, o_cmpT, gatesT)


def _out_proj_kernel(a_ref, c_ref, x_ref, wa_ref, wc_ref, g_ref, b_ref, o_ref, *, alpha):
    mix = _dot(a_ref[...], wa_ref[...]) + _dot(c_ref[...], wc_ref[...])
    o_ref[...] = _layer_norm(alpha * x_ref[...] + mix, g_ref[...], b_ref[...])


def _out_proj(attn, conv, x2d, w_attn, w_conv, g, b, *, alpha, tm):
    T, d_model = x2d.shape
    row = lambda i: (i, 0)
    return pl.pallas_call(
        functools.partial(_out_proj_kernel, alpha=alpha),
        grid=(T // tm,),
        in_specs=[pl.BlockSpec((tm, attn.shape[1]), row), pl.BlockSpec((tm, conv.shape[1]), row),
                  pl.BlockSpec((tm, d_model), row),
                  _resident(w_attn.shape), _resident(w_conv.shape),
                  _resident(g.shape), _resident(b.shape)],
        out_specs=pl.BlockSpec((tm, d_model), row),
        out_shape=jax.ShapeDtypeStruct((T, d_model), F32),
        compiler_params=_cparams(1),
        name="out_proj_deepnorm",
    )(attn, conv, x2d, w_attn, w_conv, g, b)


def _ffn_kernel(h_ref, w1_ref, w2_ref, g_ref, b_ref, o_ref, hb_ref, *, alpha):
    j = pl.program_id(1)

    @pl.when(j == 0)
    def _():
        hb_ref[...] = h_ref[...].astype(BF16)
        o_ref[...] = jnp.zeros_like(o_ref)

    a = jnp.maximum(_dot(hb_ref[...], w1_ref[...]), 0.0)
    o_ref[...] += _dot((a * a).astype(BF16), w2_ref[...])

    @pl.when(j == pl.num_programs(1) - 1)
    def _():
        o_ref[...] = _layer_norm(alpha * h_ref[...] + o_ref[...], g_ref[...], b_ref[...])


def _ffn(h, w1, w2, g, b, *, alpha, tm, tf):
    T, d_model = h.shape
    d_ff = w1.shape[1]
    return pl.pallas_call(
        functools.partial(_ffn_kernel, alpha=alpha),
        grid=(T // tm, d_ff // tf),
        in_specs=[pl.BlockSpec((tm, d_model), lambda i, j: (i, 0)),
                  pl.BlockSpec((d_model, tf), lambda i, j: (0, j)),
                  pl.BlockSpec((tf, d_model), lambda i, j: (j, 0)),
                  _resident(g.shape), _resident(b.shape)],
        out_specs=pl.BlockSpec((tm, d_model), lambda i, j: (i, 0)),
        out_shape=jax.ShapeDtypeStruct((T, d_model), F32),
        scratch_shapes=[pltpu.VMEM((tm, d_model), BF16)],
        compiler_params=pltpu.CompilerParams(dimension_semantics=("parallel", "arbitrary"),
                                             vmem_limit_bytes=VMEM_LIMIT_BYTES),
        name="ffn_deepnorm",
    )(h, w1, w2, g, b)


def _rope_tables(pos, reps):
    half = HEAD_DIM // 2
    inv_freq = ROPE_THETA ** (-jnp.arange(half, dtype=F32) / half)
    ang = pos.astype(F32)[:, None] * inv_freq[None, :]
    cos, sin = jnp.cos(ang), jnp.sin(ang)
    return jnp.tile(jnp.concatenate([cos, cos], axis=1), (1, reps)), \
        jnp.tile(jnp.concatenate([-sin, sin], axis=1), (1, reps))


def _tiles(seq):
    return dict(in_proj=min(512, seq), out_proj=min(512, seq),
                ffn_rows=min(512, seq), ffn_cols=1024)


def _mixer(x2d, w_in, cmp_pe_k, cmp_w1_k, cmp_w2_k, cmp_pe_v, cmp_w1_v, cmp_w2_v,
           conv_w, conv_b, conv_ln_g, conv_ln_b, *, batch, seq):
    tiles = _tiles(seq)
    assert seq % SEL_Q_BLOCK == 0 and seq % CMP_Q_BLOCK == 0, seq
    assert seq >= WINDOW + WIN_Q_BLOCK, seq
    assert (seq // CMP_STRIDE) % CMP_ROW_CHUNK == 0, seq
    assert 3 <= min(N_SEL, seq // SEL_BLOCK) and seq // SEL_BLOCK <= SEL_LANES, seq
    attn_w = N_Q_HEADS * HEAD_DIM
    kv_end = attn_w + 6 * KV_WIDTH
    gate_end = kv_end + N_Q_HEADS * N_BRANCH
    conv_width = (w_in.shape[1] - gate_end) // 2
    wq = w_in[:, :attn_w].astype(BF16)
    wkv = w_in[:, attn_w:kv_end].astype(BF16)
    wg = jnp.pad(w_in[:, kv_end:gate_end], ((0, 0), (0, LANES - (gate_end - kv_end)))).astype(BF16)
    wa = w_in[:, gate_end:gate_end + conv_width].astype(BF16)
    wgt = w_in[:, gate_end + conv_width:].astype(BF16)
    cos_t, sin_t = _rope_tables(jnp.arange(seq), 2)

    qT, k_cmp, v_cmp, ks_aug, vsT_aug, k_win, vwT_aug, gatesT, conv = _in_proj(
        x2d, wq, wkv, wg, wa, wgt, cos_t, sin_t,
        conv_w.reshape(CONV_KERNEL, conv_width), conv_b.reshape(1, -1),
        conv_ln_g.reshape(1, -1), conv_ln_b.reshape(1, -1),
        batch=batch, seq=seq, tm=tiles["in_proj"])

    BG = batch * N_KV_GROUPS
    n_chunks = seq // CMP_STRIDE
    chunk_w = CMP_STRIDE * HEAD_DIM
    cmp_end = jnp.arange(n_chunks) * CMP_STRIDE + CMP_BLOCK - 1
    cos_c, sin_c = _rope_tables(cmp_end, 2)
    pad_cols = lambda w2: jnp.pad(w2, ((0, 0), (0, LANES - HEAD_DIM)))
    swap = lambda w2: jnp.concatenate([w2[:, HEAD_DIM // 2:], w2[:, :HEAD_DIM // 2]], axis=1)

    def compress(t, pe, w1, w2, rotate):
        return _compress(t.reshape(BG, seq, HEAD_DIM), pe.reshape(2, chunk_w),
                         w1[:CMP_STRIDE].reshape(chunk_w, -1), w1[CMP_STRIDE:].reshape(chunk_w, -1),
                         pad_cols(w2), pad_cols(swap(w2)), cos_c, sin_c, rotate=rotate)

    kc = compress(k_cmp, cmp_pe_k, cmp_w1_k, cmp_w2_k, True)
    vcT = compress(v_cmp, cmp_pe_v, cmp_w1_v, cmp_w2_v, False)

    n_slc = seq // SEL_BLOCK
    cmp_start = jnp.arange(n_chunks) * CMP_STRIDE
    slc_start = jnp.arange(SEL_LANES) * SEL_BLOCK
    overlapT = ((cmp_start[None, :] < slc_start[:, None] + SEL_BLOCK)
                & (cmp_start[None, :] + CMP_BLOCK - 1 >= slc_start[:, None])
                & (jnp.arange(n_chunks)[None, :] < n_chunks - 1)
                & (jnp.arange(SEL_LANES)[:, None] < n_slc)).astype(F32)
    o_cmpT, sel_biasT = _cmp_attn(qT, kc, vcT, overlapT, batch=batch, seq=seq,
                                  n_top=min(N_SEL, n_slc))

    flat = lambda a: a.reshape((BG,) + a.shape[2:])
    attn = _sel_win_attn(qT, sel_biasT, flat(ks_aug), flat(vsT_aug), flat(k_win), flat(vwT_aug),
                         o_cmpT, flat(gatesT), batch=batch, seq=seq)
    return attn, conv


def kernel(x, w_in, cmp_pe_k, cmp_w1_k, cmp_w2_k, cmp_pe_v, cmp_w1_v, cmp_w2_v, conv_w, conv_b,
           conv_ln_g, conv_ln_b, w_out, ln1_g, ln1_b, w_ff1, w_ff2, ln2_g, ln2_b):
    batch, seq, d_model = x.shape
    depth = w_in.shape[0]
    alpha = (2.0 * depth) ** 0.25
    tiles = _tiles(seq)
    attn_w = N_Q_HEADS * HEAD_DIM
    h = x.reshape(batch * seq, d_model)
    for l in range(depth):
        attn, conv = _mixer(h, w_in[l], cmp_pe_k[l], cmp_w1_k[l], cmp_w2_k[l],
                            cmp_pe_v[l], cmp_w1_v[l], cmp_w2_v[l],
                            conv_w[l], conv_b[l], conv_ln_g[l], conv_ln_b[l],
                            batch=batch, seq=seq)
        h = _out_proj(attn, conv, h, w_out[l, :attn_w].astype(BF16), w_out[l, attn_w:].astype(BF16),
                      ln1_g[l].reshape(1, -1), ln1_b[l].reshape(1, -1),
                      alpha=alpha, tm=tiles["out_proj"])
        h = _ffn(h, w_ff1[l].astype(BF16), w_ff2[l].astype(BF16),
                 ln2_g[l].reshape(1, -1), ln2_b[l].reshape(1, -1),
                 alpha=alpha, tm=tiles["ffn_rows"], tf=tiles["ffn_cols"])
    return h.reshape(batch, seq, d_model)
```

```python
import functools
import math

import jax
import jax.numpy as jnp
from jax import lax
from jax.experimental import pallas as pl
from jax.experimental.pallas import tpu as pltpu

F32 = jnp.float32
BF16 = jnp.bfloat16

HEAD_DIM = 64
N_KV_GROUPS = 4
HEADS_PER_GROUP = 4
N_Q_HEADS = N_KV_GROUPS * HEADS_PER_GROUP
GROUP_WIDTH = HEADS_PER_GROUP * HEAD_DIM
N_BRANCH = 3
KV_WIDTH = N_KV_GROUPS * HEAD_DIM
CMP_BLOCK = 32
CMP_STRIDE = 16
SEL_BLOCK = 64
N_SEL = 16
WINDOW = 512
NEG_INF = -1.0e30
MASK_BIAS = -(2.0 ** 100)
CONV_KERNEL = 31
CONV_HALO = 32
CONV_ROWS = 128
ROPE_THETA = 10000.0
LN_EPS = 1e-5
LANES = 128
SUBLANES = 8
PROJ_COLS = 512
SEL_LANES = LANES
CMP_Q_BLOCK = 512
SEL_Q_BLOCK = 512
WIN_Q_BLOCK = 256
CMP_ROW_CHUNK = 64
GATES_PER_GROUP = HEADS_PER_GROUP * N_BRANCH
VMEM_LIMIT_BYTES = 56 * 1024 * 1024


def _cparams(n_axes):
    return pltpu.CompilerParams(dimension_semantics=("parallel",) * n_axes,
                                vmem_limit_bytes=VMEM_LIMIT_BYTES)


def _resident(shape):
    nd = len(shape)
    return pl.BlockSpec(shape, lambda *_: (0,) * nd, pipeline_mode=pl.Buffered(1))


def _dot(a, b):
    return jnp.dot(a, b, preferred_element_type=F32)


def _dot_split(a, b):
    a_hi, b_hi = a.astype(BF16), b.astype(BF16)
    a_lo = (a - a_hi.astype(F32)).astype(BF16)
    b_lo = (b - b_hi.astype(F32)).astype(BF16)
    return _dot(a_hi, b_hi) + (_dot(a_hi, b_lo) + _dot(a_lo, b_hi))


def _layer_norm(y, g, b):
    mu = jnp.mean(y, axis=-1, keepdims=True)
    d = y - mu
    var = jnp.mean(d * d, axis=-1, keepdims=True)
    return d * lax.rsqrt(var + LN_EPS) * g + b


def _conv_ln_swish(win_ref, r0, rows, w_ref, b_ref, g_ref, beta_ref):
    span = CONV_HALO + rows
    cols = []
    for c in range(win_ref.shape[1] // LANES):
        sl = slice(c * LANES, (c + 1) * LANES)
        win = win_ref[r0:r0 + span, sl]
        acc = jnp.zeros((rows, LANES), F32)
        for r in range(SUBLANES):
            shifted = win if r == 0 else pltpu.roll(win, span - r, 0)
            for j in range(CONV_KERNEL):
                off = CONV_HALO - (CONV_KERNEL - 1) + j
                if off % SUBLANES == r:
                    acc = acc + shifted[off - r:off - r + rows] * w_ref[j:j + 1, sl]
        cols.append(acc)
    y = jnp.concatenate(cols, axis=1) + b_ref[...]
    y = _layer_norm(y, g_ref[...], beta_ref[...])
    return y * jax.nn.sigmoid(y)


def _in_proj_kernel(x_ref, wq_ref, wkv_ref, wg_ref, wa_ref, wgt_ref, cos_ref, sin_ref,
                    cw_ref, cb_ref, cg_ref, cbeta_ref,
                    qT_ref, kc_ref, vc_ref, ks_ref, vsT_ref, kw_ref, vwT_ref, gT_ref, conv_ref,
                    win_ref, *, tm, n_row_tiles):
    @pl.when(pl.program_id(0) % n_row_tiles == 0)
    def _():
        win_ref[0:CONV_HALO, :] = jnp.zeros((CONV_HALO, win_ref.shape[1]), F32)

    xb = x_ref[...].astype(BF16)
    cos = cos_ref[...]
    sin = sin_ref[...]
    lane = lax.broadcasted_iota(jnp.int32, (tm, LANES), 1)
    first_half = (lane & (HEAD_DIM - 1)) < (HEAD_DIM // 2)

    def rope(t):
        partner = jnp.where(first_half, pltpu.roll(t, LANES - HEAD_DIM // 2, 1),
                            pltpu.roll(t, HEAD_DIM // 2, 1))
        return t * cos + partner * sin

    for c in range(wa_ref.shape[1] // PROJ_COLS):
        sl = slice(c * PROJ_COLS, (c + 1) * PROJ_COLS)
        a = _dot(xb, wa_ref[:, sl])
        gt = _dot(xb, wgt_ref[:, sl])
        win_ref[CONV_HALO:, sl] = a * jax.nn.sigmoid(gt)
    for r0 in range(0, tm, CONV_ROWS):
        conv_ref[r0:r0 + CONV_ROWS, :] = _conv_ln_swish(
            win_ref, r0, CONV_ROWS, cw_ref, cb_ref, cg_ref, cbeta_ref).astype(conv_ref.dtype)
    win_ref[0:CONV_HALO, :] = win_ref[tm:tm + CONV_HALO, :]

    q_scale = HEAD_DIM ** -0.5 * math.log2(math.e)
    for c in range(wq_ref.shape[1] // PROJ_COLS):
        sl = slice(c * PROJ_COLS, (c + 1) * PROJ_COLS)
        t = _dot(xb, wq_ref[:, sl])
        t = jnp.concatenate([rope(t[:, cc * LANES:(cc + 1) * LANES])
                             for cc in range(PROJ_COLS // LANES)], axis=1)
        qT_ref[sl, :] = (t * q_scale).T.astype(BF16)

    kv = [_dot(xb, wkv_ref[:, j * KV_WIDTH:(j + 1) * KV_WIDTH]) for j in range(6)]
    k_cmp, v_cmp, k_sel, v_sel, k_win, v_win = kv
    k_sel = jnp.concatenate([rope(k_sel[:, :LANES]), rope(k_sel[:, LANES:])], axis=1)
    k_win = jnp.concatenate([rope(k_win[:, :LANES]), rope(k_win[:, LANES:])], axis=1)
    v_selT = v_sel.T.astype(BF16)
    v_winT = v_win.T.astype(BF16)

    s_start = (pl.program_id(0) % n_row_tiles) * tm
    key_blk = (s_start + lax.broadcasted_iota(jnp.int32, (tm, SEL_LANES), 0)) // SEL_BLOCK
    onehot = (key_blk == lax.broadcasted_iota(jnp.int32, (tm, SEL_LANES), 1)).astype(BF16)
    ones = jnp.ones((HEAD_DIM, tm), BF16)
    for g in range(N_KV_GROUPS):
        sl = slice(g * HEAD_DIM, (g + 1) * HEAD_DIM)
        kc_ref[g] = k_cmp[:, sl]
        vc_ref[g] = v_cmp[:, sl]
        ks_ref[g, :, 0:HEAD_DIM] = k_sel[:, sl].astype(BF16)
        ks_ref[g, :, HEAD_DIM:] = onehot
        kw_ref[g] = k_win[:, sl].astype(BF16)
        vsT_ref[g, 0:HEAD_DIM, :] = v_selT[sl]
        vsT_ref[g, HEAD_DIM:, :] = ones
        vwT_ref[g, 0:HEAD_DIM, :] = v_winT[sl]
        vwT_ref[g, HEAD_DIM:, :] = ones

    gatesT = jax.nn.sigmoid(_dot(xb, wg_ref[...])).T
    for g in range(N_KV_GROUPS):
        gT_ref[g] = gatesT[g * GATES_PER_GROUP:(g + 1) * GATES_PER_GROUP]


def _in_proj(x2d, wq, wkv, wg, wa, wgt, cos_t, sin_t, conv_w, conv_b, conv_g, conv_beta, *,
             batch, seq, tm):
    T, d_model = x2d.shape
    n_row_tiles = seq // tm
    G = N_KV_GROUPS
    row = lambda i: (i, 0)
    tab = lambda i: (i % n_row_tiles, 0)
    per_group = lambda i: (i // n_row_tiles, 0, i % n_row_tiles, 0)
    per_group_t = lambda i: (i // n_row_tiles, 0, 0, i % n_row_tiles)

    def grp(width, dtype):
        return (jax.ShapeDtypeStruct((batch, G, seq, width), dtype),
                pl.BlockSpec((None, G, tm, width), per_group))

    def grp_t(height, dtype):
        return (jax.ShapeDtypeStruct((batch, G, height, seq), dtype),
                pl.BlockSpec((None, G, height, tm), per_group_t))

    outs = [
        (jax.ShapeDtypeStruct((wq.shape[1], T), BF16),
         pl.BlockSpec((wq.shape[1], tm), lambda i: (0, i))),
        grp(HEAD_DIM, F32), grp(HEAD_DIM, F32),
        grp(HEAD_DIM + SEL_LANES, BF16), grp_t(2 * HEAD_DIM, BF16),
        grp(HEAD_DIM, BF16), grp_t(2 * HEAD_DIM, BF16),
        grp_t(GATES_PER_GROUP, F32),
        (jax.ShapeDtypeStruct((T, wa.shape[1]), BF16), pl.BlockSpec((tm, wa.shape[1]), row)),
    ]
    return pl.pallas_call(
        functools.partial(_in_proj_kernel, tm=tm, n_row_tiles=n_row_tiles),
        grid=(T // tm,),
        in_specs=[pl.BlockSpec((tm, d_model), row),
                  _resident(wq.shape), _resident(wkv.shape), _resident(wg.shape),
                  _resident(wa.shape), _resident(wgt.shape),
                  pl.BlockSpec((tm, LANES), tab), pl.BlockSpec((tm, LANES), tab),
                  _resident(conv_w.shape), _resident(conv_b.shape),
                  _resident(conv_g.shape), _resident(conv_beta.shape)],
        out_specs=[o[1] for o in outs],
        out_shape=[o[0] for o in outs],
        scratch_shapes=[pltpu.VMEM((CONV_HALO + tm, wa.shape[1]), F32)],
        compiler_params=pltpu.CompilerParams(dimension_semantics=("arbitrary",),
                                             vmem_limit_bytes=VMEM_LIMIT_BYTES),
        name="in_proj",
    )(x2d, wq, wkv, wg, wa, wgt, cos_t, sin_t, conv_w, conv_b, conv_g, conv_beta)


def _compress_kernel(c_ref, pe_ref, w1a_ref, w1b_ref, w2_ref, w2s_ref, cos_ref, sin_ref,
                     o_ref, *, rotate):
    n_chunks = c_ref.shape[0] // CMP_STRIDE
    c = jnp.concatenate([c_ref[pl.ds(l, n_chunks, stride=CMP_STRIDE), :]
                         for l in range(CMP_STRIDE)], axis=1)
    a = _dot_split(c + pe_ref[0:1, :], w1a_ref[...])
    b = _dot_split(c + pe_ref[1:2, :], w1b_ref[...])
    hid = jax.nn.gelu(a + pltpu.roll(b, n_chunks - 1, 0))
    out = _dot_split(hid, w2_ref[...])
    if rotate:
        out = out * cos_ref[...] + _dot_split(hid, w2s_ref[...]) * sin_ref[...]
        o_ref[...] = out[:, :HEAD_DIM].astype(o_ref.dtype)
    else:
        o_ref[...] = out.T[:HEAD_DIM].astype(o_ref.dtype)


def _compress(tokens, pe2, w1a, w1b, w2, w2s, cos_c, sin_c, *, rotate):
    BG, seq, _ = tokens.shape
    n_chunks = seq // CMP_STRIDE
    out_block = (None, n_chunks, HEAD_DIM) if rotate else (None, HEAD_DIM, n_chunks)
    return pl.pallas_call(
        functools.partial(_compress_kernel, rotate=rotate),
        grid=(BG,),
        in_specs=[pl.BlockSpec((None, seq, HEAD_DIM), lambda i: (i, 0, 0)),
                  _resident(pe2.shape), _resident(w1a.shape), _resident(w1b.shape),
                  _resident(w2.shape), _resident(w2s.shape),
                  _resident(cos_c.shape), _resident(sin_c.shape)],
        out_specs=pl.BlockSpec(out_block, lambda i: (i, 0, 0)),
        out_shape=jax.ShapeDtypeStruct((BG,) + out_block[1:], BF16),
        compiler_params=_cparams(1),
        name="compress_k" if rotate else "compress_v",
    )(tokens, pe2, w1a, w1b, w2, w2s, cos_c, sin_c)


def _heads_to_lanes(qT):
    return jnp.concatenate(
        [qT[h * HEAD_DIM:(h + 1) * HEAD_DIM] for h in range(HEADS_PER_GROUP)], axis=1)


def _tile_heads(a):
    return jnp.concatenate([a] * HEADS_PER_GROUP, axis=1)


def _window_attention(qT, s0, kw_ref, vwT_ref):
    wq = WIN_Q_BLOCK
    d = (lax.broadcasted_iota(jnp.int32, (wq, wq), 0)
         - lax.broadcasted_iota(jnp.int32, (wq, wq), 1))
    subs = []
    for sub in range(qT.shape[1] // wq):
        sq = s0 + sub * wq
        q_sub = _heads_to_lanes(qT[:, sub * wq:(sub + 1) * wq])
        wstart = pl.multiple_of(jnp.maximum(sq - WINDOW, 0), wq)
        sw = []
        for c in range(WINDOW // wq + 1):
            off = sq - wstart - c * wq
            wbias = jnp.where((d <= off) & (d > off - WINDOW), 0.0, NEG_INF)
            sw.append(_dot(kw_ref[pl.ds(wstart + c * wq, wq), :], q_sub) + _tile_heads(wbias))
        sw = jnp.concatenate(sw, axis=0)
        pw = jnp.exp2(sw - jnp.max(sw, axis=0, keepdims=True)).astype(BF16)
        accw = _dot(vwT_ref[:, pl.ds(wstart, WINDOW + wq)], pw)
        subs.append(accw[:HEAD_DIM] * (1.0 / accw[HEAD_DIM:HEAD_DIM + 1]))
    return [jnp.concatenate([o[:, h * wq:(h + 1) * wq] for o in subs], axis=1)
            for h in range(HEADS_PER_GROUP)]


def _cmp_attn_kernel(qT_ref, kc_ref, vcT_ref, ovT_ref, ocT_ref, sbT_ref, *, n_top):
    qblk = CMP_Q_BLOCK
    qb = pl.program_id(1)
    s0 = qb * qblk
    cols = HEADS_PER_GROUP * qblk
    n_cmp = kc_ref.shape[0]
    q_stackT = _heads_to_lanes(qT_ref[...])
    t_lane = s0 + (lax.broadcasted_iota(jnp.int32, (1, cols), 1) & (qblk - 1))

    def attend(n):
        base = (lax.broadcasted_iota(jnp.int32, (n, qblk), 0) * CMP_STRIDE
                - lax.broadcasted_iota(jnp.int32, (n, qblk), 1))
        bias = jnp.where(base <= s0 - (CMP_BLOCK - 1), 0.0, NEG_INF)
        s = _dot(kc_ref[0:n, :], q_stackT) + _tile_heads(bias)
        m = jnp.max(s, axis=0, keepdims=True)
        e = jnp.exp2(s - m)
        den = jnp.sum(e, axis=0, keepdims=True)
        p = e * jnp.where(t_lane >= CMP_BLOCK - 1, 1.0 / den, 0.0)
        o = _dot(vcT_ref[:, 0:n], p.astype(BF16))
        for h in range(HEADS_PER_GROUP):
            ocT_ref[h] = o[:, h * qblk:(h + 1) * qblk]
        p_sum = p[:, 0:qblk]
        for h in range(1, HEADS_PER_GROUP):
            p_sum = p_sum + p[:, h * qblk:(h + 1) * qblk]
        n_sel = n * CMP_STRIDE // SEL_BLOCK
        imp = _dot_split(ovT_ref[0:n_sel, 0:n], p_sum)

        tq = s0 + lax.broadcasted_iota(jnp.int32, (n_sel, qblk), 1)
        blk = lax.broadcasted_iota(jnp.int32, (n_sel, qblk), 0)
        cur = tq // SEL_BLOCK
        forced = (blk == 0) | (blk == cur) | (blk == cur - 1)
        v = jnp.where(forced, -jnp.inf, jnp.where(blk * SEL_BLOCK <= tq, imp, -1.0))
        idx = blk.astype(F32)
        for _ in range(n_top - 3):
            vmax = jnp.max(v, axis=0, keepdims=True)
            first = jnp.min(jnp.where(v == vmax, idx, float(SEL_LANES)), axis=0, keepdims=True)
            v = jnp.where(idx == first, -jnp.inf, v)
        sbT_ref[0:n_sel, :] = jnp.where(v == -jnp.inf, 0.0, MASK_BIAS).astype(BF16)
        if n_sel < SEL_LANES:
            sbT_ref[n_sel:, :] = jnp.full((SEL_LANES - n_sel, qblk), MASK_BIAS, BF16)

    chunks_needed = pl.cdiv((qb + 1) * (qblk // CMP_STRIDE), CMP_ROW_CHUNK)
    for k in range(1, n_cmp // CMP_ROW_CHUNK + 1):
        pl.when(chunks_needed == k)(functools.partial(attend, k * CMP_ROW_CHUNK))


def _cmp_attn(qT, kc, vcT, overlapT, *, batch, seq, n_top):
    BG, n_cmp, _ = kc.shape
    G = N_KV_GROUPS
    qblk = CMP_Q_BLOCK
    nqb = seq // qblk
    return pl.pallas_call(
        functools.partial(_cmp_attn_kernel, n_top=n_top),
        grid=(BG, nqb),
        in_specs=[pl.BlockSpec((GROUP_WIDTH, qblk),
                               lambda bg, qb: (bg % G, (bg // G) * nqb + qb)),
                  pl.BlockSpec((None, n_cmp, HEAD_DIM), lambda bg, qb: (bg, 0, 0)),
                  pl.BlockSpec((None, HEAD_DIM, n_cmp), lambda bg, qb: (bg, 0, 0)),
                  _resident(overlapT.shape)],
        out_specs=[pl.BlockSpec((None, HEADS_PER_GROUP, HEAD_DIM, qblk),
                                lambda bg, qb: (bg, 0, 0, qb)),
                   pl.BlockSpec((None, SEL_LANES, qblk), lambda bg, qb: (bg, 0, qb))],
        out_shape=[jax.ShapeDtypeStruct((BG, HEADS_PER_GROUP, HEAD_DIM, seq), F32),
                   jax.ShapeDtypeStruct((BG, SEL_LANES, seq), BF16)],
        compiler_params=_cparams(2),
        name="cmp_attn_select",
    )(qT, kc, vcT, overlapT)


def _sel_win_kernel(qT_ref, sbT_ref, ks_ref, vsT_ref, kw_ref, vwT_ref, ocT_ref, gT_ref, o_ref,
                    s_ref, p_ref):
    qblk = SEL_Q_BLOCK
    s0 = pl.program_id(1) * qblk
    cols = HEADS_PER_GROUP * qblk
    qT = qT_ref[...]
    biasT = sbT_ref[...]
    q_augT = jnp.concatenate(
        [jnp.concatenate([qT[h * HEAD_DIM:(h + 1) * HEAD_DIM], biasT], axis=0)
         for h in range(HEADS_PER_GROUP)], axis=1)

    def tile_start(kt):
        return pl.multiple_of(kt * qblk, qblk)

    def scores(kt):
        return _dot(ks_ref[pl.ds(tile_start(kt), qblk), :], q_augT)

    def pv(kt, p):
        return _dot(vsT_ref[:, pl.ds(tile_start(kt), qblk)], p)

    def softmax_step(s, kt, m, masked):
        if masked:
            d = (lax.broadcasted_iota(jnp.int32, (qblk, qblk), 0)
                 - lax.broadcasted_iota(jnp.int32, (qblk, qblk), 1))
            s = s + _tile_heads(jnp.where(d <= s0 - kt * qblk, 0.0, MASK_BIAS))
        m_new = jnp.maximum(m, jnp.max(s, axis=0, keepdims=True))
        return m_new, jnp.exp2(m - m_new), jnp.exp2(s - m_new).astype(BF16)

    def pair(i, carry):
        m, acc = carry
        pv_prev = pv(jnp.maximum(2 * i - 1, 0), p_ref[...])
        s_odd = scores(2 * i + 1)
        m, alpha, p = softmax_step(s_ref[...], 2 * i, m, False)
        acc = alpha * (acc + pv_prev) + pv(2 * i, p)
        s_ref[...] = scores(2 * i + 2)
        m, alpha, p = softmax_step(s_odd, 2 * i + 1, m, False)
        p_ref[...] = p
        return m, alpha * acc

    def tail(kt, n_full, carry):
        m, acc = carry
        pv_prev = pv(jnp.maximum(kt - 1, 0), p_ref[...])
        m, alpha, p = softmax_step(s_ref[...], kt, m, True)
        acc = alpha * (acc + pv_prev) + pv(kt, p)

        def diagonal(c):
            m, acc = c
            m, alpha, p = softmax_step(scores(kt + 1), kt + 1, m, True)
            return m, alpha * acc + pv(kt + 1, p)

        return lax.cond(kt < n_full, diagonal, lambda c: c, (m, acc))

    s_ref[...] = scores(0)
    p_ref[...] = jnp.zeros_like(p_ref)
    o_w = _window_attention(qT, s0, kw_ref, vwT_ref)
    n_full = pl.program_id(1)
    n_main = n_full // 2
    init = (jnp.full((1, cols), NEG_INF, F32), jnp.zeros((2 * HEAD_DIM, cols), F32))
    carry = lax.fori_loop(0, n_main, pair, init)
    _, acc = tail(2 * n_main, n_full, carry)
    o_s = acc[:HEAD_DIM] * (1.0 / acc[HEAD_DIM:HEAD_DIM + 1])

    gT = gT_ref[...]
    outs = []
    for h in range(HEADS_PER_GROUP):
        c = slice(h * qblk, (h + 1) * qblk)
        gate = lambda br: gT[h * N_BRANCH + br:h * N_BRANCH + br + 1, :]
        outs.append(gate(0) * ocT_ref[h] + gate(1) * o_s[:, c] + gate(2) * o_w[h])
    o_ref[...] = jnp.concatenate(outs, axis=0).T.astype(o_ref.dtype)


def _sel_win_attn(qT, sel_biasT, ks_aug, vsT_aug, kw, vwT_aug, o_cmpT, gatesT, *,
                  batch, seq):
    G = N_KV_GROUPS
    BG = batch * G
    qblk = SEL_Q_BLOCK
    nqb = seq // qblk
    whole = lambda a: pl.BlockSpec((None,) + a.shape[1:], lambda bg, qb: (bg, 0, 0),
                                   pipeline_mode=pl.Buffered(1))
    return pl.pallas_call(
        _sel_win_kernel,
        grid=(BG, nqb),
        in_specs=[pl.BlockSpec((GROUP_WIDTH, qblk),
                               lambda bg, qb: (bg % G, (bg // G) * nqb + qb)),
                  pl.BlockSpec((None, SEL_LANES, qblk), lambda bg, qb: (bg, 0, qb)),
                  whole(ks_aug), whole(vsT_aug), whole(kw), whole(vwT_aug),
                  pl.BlockSpec((None, HEADS_PER_GROUP, HEAD_DIM, qblk),
                               lambda bg, qb: (bg, 0, 0, qb)),
                  pl.BlockSpec((None, GATES_PER_GROUP, qblk), lambda bg, qb: (bg, 0, qb))],
        out_specs=pl.BlockSpec((qblk, GROUP_WIDTH),
                               lambda bg, qb: ((bg // G) * nqb + qb, bg % G)),
        out_shape=jax.ShapeDtypeStruct((batch * seq, N_Q_HEADS * HEAD_DIM), BF16),
        scratch_shapes=[pltpu.VMEM((qblk, HEADS_PER_GROUP * qblk), F32),
                        pltpu.VMEM((qblk, HEADS_PER_GROUP * qblk), BF16)],
        compiler_params=_cparams(2),
        name="sel_win_attn",
    )(qT, sel_biasT, ks_aug, vsT_aug, kw, vwT_aug, o_cmpT, gatesT)


def _out_proj_kernel(a_ref, c_ref, x_ref, wa_ref, wc_ref, g_ref, b_ref, o_ref, *, alpha):
    mix = _dot(a_ref[...], wa_ref[...]) + _dot(c_ref[...], wc_ref[...])
    o_ref[...] = _layer_norm(alpha * x_ref[...] + mix, g_ref[...], b_ref[...])


def _out_proj(attn, conv, x2d, w_attn, w_conv, g, b, *, alpha, tm):
    T, d_model = x2d.shape
    row = lambda i: (i, 0)
    return pl.pallas_call(
        functools.partial(_out_proj_kernel, alpha=alpha),
        grid=(T // tm,),
        in_specs=[pl.BlockSpec((tm, attn.shape[1]), row), pl.BlockSpec((tm, conv.shape[1]), row),
                  pl.BlockSpec((tm, d_model), row),
                  _resident(w_attn.shape), _resident(w_conv.shape),
                  _resident(g.shape), _resident(b.shape)],
        out_specs=pl.BlockSpec((tm, d_model), row),
        out_shape=jax.ShapeDtypeStruct((T, d_model), F32),
        compiler_params=_cparams(1),
        name="out_proj_deepnorm",
    )(attn, conv, x2d, w_attn, w_conv, g, b)


def _ffn_kernel(h_ref, w1_ref, w2_ref, g_ref, b_ref, o_ref, hb_ref, *, alpha):
    j = pl.program_id(1)

    @pl.when(j == 0)
    def _():
        hb_ref[...] = h_ref[...].astype(BF16)
        o_ref[...] = jnp.zeros_like(o_ref)

    a = jnp.maximum(_dot(hb_ref[...], w1_ref[...]), 0.0)
    o_ref[...] += _dot((a * a).astype(BF16), w2_ref[...])

    @pl.when(j == pl.num_programs(1) - 1)
    def _():
        o_ref[...] = _layer_norm(alpha * h_ref[...] + o_ref[...], g_ref[...], b_ref[...])


def _ffn(h, w1, w2, g, b, *, alpha, tm, tf):
    T, d_model = h.shape
    d_ff = w1.shape[1]
    return pl.pallas_call(
        functools.partial(_ffn_kernel, alpha=alpha),
        grid=(T // tm, d_ff // tf),
        in_specs=[pl.BlockSpec((tm, d_model), lambda i, j: (i, 0)),
                  pl.BlockSpec((d_model, tf), lambda i, j: (0, j)),
                  pl.BlockSpec((tf, d_model), lambda i, j: (j, 0)),
                  _resident(g.shape), _resident(b.shape)],
        out_specs=pl.BlockSpec((tm, d_model), lambda i, j: (i, 0)),
        out_shape=jax.ShapeDtypeStruct((T, d_model), F32),
        scratch_shapes=[pltpu.VMEM((tm, d_model), BF16)],
        compiler_params=pltpu.CompilerParams(dimension_semantics=("parallel", "arbitrary"),
                                             vmem_limit_bytes=VMEM_LIMIT_BYTES),
        name="ffn_deepnorm",
    )(h, w1, w2, g, b)


def _rope_tables(pos, reps):
    half = HEAD_DIM // 2
    inv_freq = ROPE_THETA ** (-jnp.arange(half, dtype=F32) / half)
    ang = pos.astype(F32)[:, None] * inv_freq[None, :]
    cos, sin = jnp.cos(ang), jnp.sin(ang)
    return jnp.tile(jnp.concatenate([cos, cos], axis=1), (1, reps)), \
        jnp.tile(jnp.concatenate([-sin, sin], axis=1), (1, reps))


def _tiles(seq):
    return dict(in_proj=min(512, seq), out_proj=min(512, seq),
                ffn_rows=min(512, seq), ffn_cols=1024)


def _mixer(x2d, w_in, cmp_pe_k, cmp_w1_k, cmp_w2_k, cmp_pe_v, cmp_w1_v, cmp_w2_v,
           conv_w, conv_b, conv_ln_g, conv_ln_b, *, batch, seq):
    tiles = _tiles(seq)
    assert seq % SEL_Q_BLOCK == 0 and seq % CMP_Q_BLOCK == 0, seq
    assert seq >= WINDOW + WIN_Q_BLOCK, seq
    assert (seq // CMP_STRIDE) % CMP_ROW_CHUNK == 0, seq
    assert 3 <= min(N_SEL, seq // SEL_BLOCK) and seq // SEL_BLOCK <= SEL_LANES, seq
    attn_w = N_Q_HEADS * HEAD_DIM
    kv_end = attn_w + 6 * KV_WIDTH
    gate_end = kv_end + N_Q_HEADS * N_BRANCH
    conv_width = (w_in.shape[1] - gate_end) // 2
    wq = w_in[:, :attn_w].astype(BF16)
    wkv = w_in[:, attn_w:kv_end].astype(BF16)
    wg = jnp.pad(w_in[:, kv_end:gate_end], ((0, 0), (0, LANES - (gate_end - kv_end)))).astype(BF16)
    wa = w_in[:, gate_end:gate_end + conv_width].astype(BF16)
    wgt = w_in[:, gate_end + conv_width:].astype(BF16)
    cos_t, sin_t = _rope_tables(jnp.arange(seq), 2)

    qT, k_cmp, v_cmp, ks_aug, vsT_aug, k_win, vwT_aug, gatesT, conv = _in_proj(
        x2d, wq, wkv, wg, wa, wgt, cos_t, sin_t,
        conv_w.reshape(CONV_KERNEL, conv_width), conv_b.reshape(1, -1),
        conv_ln_g.reshape(1, -1), conv_ln_b.reshape(1, -1),
        batch=batch, seq=seq, tm=tiles["in_proj"])

    BG = batch * N_KV_GROUPS
    n_chunks = seq // CMP_STRIDE
    chunk_w = CMP_STRIDE * HEAD_DIM
    cmp_end = jnp.arange(n_chunks) * CMP_STRIDE + CMP_BLOCK - 1
    cos_c, sin_c = _rope_tables(cmp_end, 2)
    pad_cols = lambda w2: jnp.pad(w2, ((0, 0), (0, LANES - HEAD_DIM)))
    swap = lambda w2: jnp.concatenate([w2[:, HEAD_DIM // 2:], w2[:, :HEAD_DIM // 2]], axis=1)

    def compress(t, pe, w1, w2, rotate):
        return _compress(t.reshape(BG, seq, HEAD_DIM), pe.reshape(2, chunk_w),
                         w1[:CMP_STRIDE].reshape(chunk_w, -1), w1[CMP_STRIDE:].reshape(chunk_w, -1),
                         pad_cols(w2), pad_cols(swap(w2)), cos_c, sin_c, rotate=rotate)

    kc = compress(k_cmp, cmp_pe_k, cmp_w1_k, cmp_w2_k, True)
    vcT = compress(v_cmp, cmp_pe_v, cmp_w1_v, cmp_w2_v, False)

    n_slc = seq // SEL_BLOCK
    cmp_start = jnp.arange(n_chunks) * CMP_STRIDE
    slc_start = jnp.arange(SEL_LANES) * SEL_BLOCK
    overlapT = ((cmp_start[None, :] < slc_start[:, None] + SEL_BLOCK)
                & (cmp_start[None, :] + CMP_BLOCK - 1 >= slc_start[:, None])
                & (jnp.arange(n_chunks)[None, :] < n_chunks - 1)
                & (jnp.arange(SEL_LANES)[:, None] < n_slc)).astype(F32)
    o_cmpT, sel_biasT = _cmp_attn(qT, kc, vcT, overlapT, batch=batch, seq=seq,
                                  n_top=min(N_SEL, n_slc))

    flat = lambda a: a.reshape((BG,) + a.shape[2:])
    attn = _sel_win_attn(qT, sel_biasT, flat(ks_aug), flat(vsT_aug), flat(k_win), flat(vwT_aug),
                         o_cmpT, flat(gatesT), batch=batch, seq=seq)
    return attn, conv


def kernel(x, w_in, cmp_pe_k, cmp_w1_k, cmp_w2_k, cmp_pe_v, cmp_w1_v, cmp_w2_v, conv_w, conv_b,
           conv_ln_g, conv_ln_b, w_out, ln1_g, ln1_b, w_ff1, w_ff2, ln2_g, ln2_b):
    batch, seq, d_model = x.shape
    depth = w_in.shape[0]
    alpha = (2.0 * depth) ** 0.25
    tiles = _tiles(seq)
    attn_w = N_Q_HEADS * HEAD_DIM
    h = x.reshape(batch * seq, d_model)
    for l in range(depth):
        attn, conv = _mixer(h, w_in[l], cmp_pe_k[l], cmp_w1_k[l], cmp_w2_k[l],
                            cmp_pe_v[l], cmp_w1_v[l], cmp_w2_v[l],
                            conv_w[l], conv_b[l], conv_ln_g[l], conv_ln_b[l],
                            batch=batch, seq=seq)
        h = _out_proj(attn, conv, h, w_out[l, :attn_w].astype(BF16), w_out[l, attn_w:].astype(BF16),
                      ln1_g[l].reshape(1, -1), ln1_b[l].reshape(1, -1),
                      alpha=alpha, tm=tiles["out_proj"])
        h = _ffn(h, w_ff1[l].astype(BF16), w_ff2[l].astype(BF16),
                 ln2_g[l].reshape(1, -1), ln2_b[l].reshape(1, -1),
                 alpha=alpha, tm=tiles["ffn_rows"], tf=tiles["ffn_cols"])
    return h.reshape(batch, seq, d_model)
```
